```python
import math
import jax, jax.numpy as jnp
from jax import lax
import numpy as np

D_MODEL = 2048
BATCH = 1
SEQ = 16384
DEPTH = 1

N_HEADS = 8
HEAD_DIM = 128
ATTN_W = N_HEADS * HEAD_DIM
MOBA_BLOCK = 256
MOBA_TOPK = 3
Q_CHUNK = 64
POOL_WINDOWS = (2, 4, 8, 16)
POOL_GROUPS = len(POOL_WINDOWS)
POOL_W = 1024
POOL_GW = POOL_W // POOL_GROUPS
IN_W = 3 * ATTN_W + POOL_W
N_BRANCH = 2
D_FF = 5632
CONV_W = 3
ROPE_THETA = 10000.0
EPS = 1e-6
NEG = -1e30

kernel_name = "hybrid_moba_pool_convffn_block"


def rmsnorm(x, g):
    x32 = x.astype(jnp.float32)
    y = x32 * lax.rsqrt(jnp.mean(x32 * x32, axis=-1, keepdims=True) + EPS)
    return (y * g.astype(jnp.float32)).astype(x.dtype)


def modulate(h, shift, scale):
    return h * (1 + scale[:, None, :]) + shift[:, None, :]


def rope(t, pos):
    half = t.shape[-1] // 2
    inv = ROPE_THETA ** (-jnp.arange(half, dtype=jnp.float32) / half)
    ang = pos.astype(jnp.float32)[:, None] * inv[None, :]
    cos, sin = jnp.cos(ang), jnp.sin(ang)
    t32 = t.astype(jnp.float32)
    t1, t2 = t32[..., :half], t32[..., half:]
    return jnp.concatenate([t1 * cos - t2 * sin, t2 * cos + t1 * sin], axis=-1).astype(t.dtype)


def moba_attention(q, k, v):
    B, H, S, hd = q.shape
    nb = -(-S // MOBA_BLOCK)
    pad = nb * MOBA_BLOCK - S
    k_blk = jnp.pad(k, ((0, 0), (0, 0), (0, pad), (0, 0))).reshape(B, H, nb, MOBA_BLOCK, hd)
    v_blk = jnp.pad(v, ((0, 0), (0, 0), (0, pad), (0, 0))).reshape(B, H, nb, MOBA_BLOCK, hd)
    k_mean = jnp.mean(k_blk.astype(jnp.float32), axis=3)
    topk = min(MOBA_TOPK, nb)
    n_chunks = S // Q_CHUNK
    scale = hd ** -0.5
    b_idx = jnp.arange(B)[:, None, None, None]
    h_idx = jnp.arange(H)[None, :, None, None]
    blk_ids = jnp.arange(nb)

    def one_chunk(ci):
        q0 = ci * Q_CHUNK
        own = q0 // MOBA_BLOCK
        qc = lax.dynamic_slice_in_dim(q, q0, Q_CHUNK, axis=2).astype(jnp.float32) * scale
        qpos = q0 + jnp.arange(Q_CHUNK)
        gate = jnp.einsum('bhqd,bhnd->bhqn', qc, k_mean)
        gate = jnp.where(blk_ids[None, None, None, :] < own, gate, NEG)
        _, sel = lax.top_k(gate, topk)
        valid = sel < own
        k_sel = k_blk[b_idx, h_idx, sel].astype(jnp.float32)
        v_sel = v_blk[b_idx, h_idx, sel].astype(jnp.float32)
        s_sel = jnp.einsum('bhqd,bhqjkd->bhqjk', qc, k_sel)
        s_sel = jnp.where(valid[..., None], s_sel, NEG).reshape(B, H, Q_CHUNK, topk * MOBA_BLOCK)
        k_own = lax.dynamic_index_in_dim(k_blk, own, axis=2, keepdims=False).astype(jnp.float32)
        v_own = lax.dynamic_index_in_dim(v_blk, own, axis=2, keepdims=False).astype(jnp.float32)
        s_own = jnp.einsum('bhqd,bhkd->bhqk', qc, k_own)
        kpos = own * MOBA_BLOCK + jnp.arange(MOBA_BLOCK)
        s_own = jnp.where(kpos[None, :] <= qpos[:, None], s_own, NEG)
        p = jax.nn.softmax(jnp.concatenate([s_sel, s_own], axis=-1), axis=-1)
        p_sel = p[..., :topk * MOBA_BLOCK].reshape(B, H, Q_CHUNK, topk, MOBA_BLOCK)
        p_own = p[..., topk * MOBA_BLOCK:]
        o = (jnp.einsum('bhqjk,bhqjkd->bhqd', p_sel, v_sel)
             + jnp.einsum('bhqk,bhkd->bhqd', p_own, v_own))
        return o.astype(q.dtype)

    out = lax.map(one_chunk, jnp.arange(n_chunks))
    return out.transpose(1, 2, 0, 3, 4).reshape(B, H, S, hd)


def multiscale_pool(u, w_grp, ls):
    B, S, C = u.shape
    u32 = u.astype(jnp.float32)
    cs0 = jnp.concatenate([jnp.zeros((B, 1, C), jnp.float32), jnp.cumsum(u32, axis=1)], axis=1)
    t = jnp.arange(S)
    pooled = []
    for g, w in enumerate(POOL_WINDOWS):
        sl = cs0[..., g * POOL_GW:(g + 1) * POOL_GW]
        lag = jnp.concatenate([jnp.zeros((B, w - 1, POOL_GW), jnp.float32), sl[:, :S - w + 1]], axis=1)
        cnt = jnp.minimum(t + 1, w).astype(jnp.float32)[None, :, None]
        pooled.append((sl[:, 1:] - lag) / cnt)
    d = (jnp.concatenate(pooled, axis=-1) - u32).astype(u.dtype).reshape(B, S, POOL_GROUPS, POOL_GW)
    mixed = jnp.einsum('bsgc,gcd->bsgd', d, w_grp).reshape(B, S, C)
    return mixed * ls


def hybrid_mixer(h, pos, w_in, q_norm_g, k_norm_g, w_pool_grp, pool_scale,
                 w_attn_br, w_pool_br, w_gate, b_gate, w_o):
    B, S, _ = h.shape
    proj = h @ w_in
    q = proj[..., :ATTN_W]
    k = proj[..., ATTN_W:2 * ATTN_W]
    v = proj[..., 2 * ATTN_W:3 * ATTN_W]
    u = proj[..., 3 * ATTN_W:]
    to_heads = lambda t: t.reshape(B, S, N_HEADS, HEAD_DIM).transpose(0, 2, 1, 3)
    q = rope(rmsnorm(to_heads(q), q_norm_g), pos)
    k = rope(rmsnorm(to_heads(k), k_norm_g), pos)
    v = to_heads(v)
    attn = moba_attention(q, k, v).transpose(0, 2, 1, 3).reshape(B, S, ATTN_W)
    y_attn = attn @ w_attn_br
    y_pool = multiscale_pool(u, w_pool_grp, pool_scale) @ w_pool_br
    gates = jax.nn.sigmoid(h @ w_gate + b_gate)
    merged = gates[..., :D_MODEL] * y_attn + gates[..., D_MODEL:] * y_pool
    return merged @ w_o


def conv_glu_ffn(h, w_up, conv_w, conv_b, w_down):
    S = h.shape[1]
    up = h @ w_up
    up_pad = jnp.pad(up, ((0, 0), (CONV_W - 1, 0), (0, 0)))
    conv = conv_b + sum(conv_w[j] * up_pad[:, j:j + S] for j in range(CONV_W))
    a, b = conv[..., :D_FF], conv[..., D_FF:]
    return (jax.nn.silu(a) * b) @ w_down


def setup_inputs(seed: int = 0) -> dict:
    key = jax.random.key(seed)
    ks = jax.random.split(key, 20)
    nrm = lambda k, shape, s: jax.random.normal(k, shape, jnp.float32) * s
    L = DEPTH
    return {
        "x": nrm(ks[0], (BATCH, SEQ, D_MODEL), 1.0),
        "c": nrm(ks[1], (BATCH, D_MODEL), 1.0),
        "w_ada": nrm(ks[2], (L, D_MODEL, 6 * D_MODEL), 0.1 * D_MODEL ** -0.5),
        "b_ada": nrm(ks[3], (L, 6 * D_MODEL), 0.02),
        "norm_mix_g": 1.0 + nrm(ks[4], (L, D_MODEL), 0.02),
        "w_in": nrm(ks[5], (L, D_MODEL, IN_W), D_MODEL ** -0.5),
        "q_norm_g": 1.0 + nrm(ks[6], (L, HEAD_DIM), 0.02),
        "k_norm_g": 1.0 + nrm(ks[7], (L, HEAD_DIM), 0.02),
        "w_pool_grp": nrm(ks[8], (L, POOL_GROUPS, POOL_GW, POOL_GW), POOL_GW ** -0.5),
        "pool_scale": 1.0 + nrm(ks[9], (L, POOL_W), 0.02),
        "w_attn_br": nrm(ks[10], (L, ATTN_W, D_MODEL), ATTN_W ** -0.5),
        "w_pool_br": nrm(ks[11], (L, POOL_W, D_MODEL), POOL_W ** -0.5),
        "w_gate": nrm(ks[12], (L, D_MODEL, N_BRANCH * D_MODEL), D_MODEL ** -0.5),
        "b_gate": nrm(ks[13], (L, N_BRANCH * D_MODEL), 0.02),
        "w_o": nrm(ks[14], (L, D_MODEL, D_MODEL), D_MODEL ** -0.5),
        "norm_ffn_g": 1.0 + nrm(ks[15], (L, D_MODEL), 0.02),
        "w_up": nrm(ks[16], (L, D_MODEL, 2 * D_FF), D_MODEL ** -0.5),
        "conv_w": nrm(ks[17], (L, CONV_W, 2 * D_FF), CONV_W ** -0.5),
        "conv_b": nrm(ks[18], (L, 2 * D_FF), 0.02),
        "w_down": nrm(ks[19], (L, D_FF, D_MODEL), D_FF ** -0.5),
    }


def reference(x, c, w_ada, b_ada, norm_mix_g, w_in, q_norm_g, k_norm_g, w_pool_grp,
              pool_scale, w_attn_br, w_pool_br, w_gate, b_gate, w_o, norm_ffn_g,
              w_up, conv_w, conv_b, w_down):
    S = x.shape[1]
    pos = jnp.arange(S)
    for l in range(DEPTH):
        mod = jax.nn.silu(c) @ w_ada[l] + b_ada[l]
        sh1, sc1, g1, sh2, sc2, g2 = jnp.split(mod, 6, axis=-1)
        h = modulate(rmsnorm(x, norm_mix_g[l]), sh1, sc1)
        x = x + g1[:, None, :] * hybrid_mixer(
            h, pos, w_in[l], q_norm_g[l], k_norm_g[l], w_pool_grp[l], pool_scale[l],
            w_attn_br[l], w_pool_br[l], w_gate[l], b_gate[l], w_o[l])
        h = modulate(rmsnorm(x, norm_ffn_g[l]), sh2, sc2)
        x = x + g2[:, None, :] * conv_glu_ffn(h, w_up[l], conv_w[l], conv_b[l], w_down[l])
    return x
```

```python
import functools

import jax
import jax.numpy as jnp
from jax import lax
from jax.experimental import pallas as pl
from jax.experimental.pallas import tpu as pltpu

D_MODEL = 2048
N_HEADS = 8
HEAD_DIM = 128
ATTN_W = N_HEADS * HEAD_DIM
MOBA_BLOCK = 256
MOBA_TOPK = 3
POOL_WINDOWS = (2, 4, 8, 16)
POOL_W = 1024
POOL_GW = POOL_W // len(POOL_WINDOWS)
D_FF = 5632
CONV_W = 3
ROPE_THETA = 10000.0
EPS = 1e-6
NEG = -1e30
BLOCK_SHIFT = MOBA_BLOCK.bit_length() - 1

LANES = 128
SUBLANES = 8
PROJ_W = 4 * ATTN_W + 2 * D_MODEL
Q_COL, K_COL, V_COL, U_COL, G_COL = 0, ATTN_W, 2 * ATTN_W, 3 * ATTN_W, 4 * ATTN_W

MIB = 1024 * 1024


def _dot(a, b):
    return jnp.dot(a, b, preferred_element_type=jnp.float32)


def _dot_nt(a, b):
    return lax.dot_general(a, b, (((1,), (1,)), ((), ())), preferred_element_type=jnp.float32)


def _params(semantics, vmem_mib):
    return pltpu.CompilerParams(dimension_semantics=semantics, vmem_limit_bytes=vmem_mib * MIB)


def _norm_modulate(x, g, shift, scale):
    ms = jnp.mean(x * x, axis=-1, keepdims=True)
    y = x * lax.rsqrt(ms + EPS) * g
    return y * (1.0 + scale) + shift


def _ada_kernel(c_ref, w_ref, b_ref, o_ref):
    c = c_ref[...]
    s = c * jax.nn.sigmoid(c)
    o_ref[...] = jnp.sum(s * w_ref[...], axis=0, keepdims=True) + b_ref[...]


def _ada(c_col, w_ada, b_ada):
    d, n = w_ada.shape
    tn = 1024
    return pl.pallas_call(
        _ada_kernel,
        out_shape=jax.ShapeDtypeStruct((1, n), jnp.float32),
        grid=(n // tn,),
        in_specs=[pl.BlockSpec((d, 1), lambda j: (0, 0)),
                  pl.BlockSpec((d, tn), lambda j: (0, j)),
                  pl.BlockSpec((1, tn), lambda j: (0, j))],
        out_specs=pl.BlockSpec((1, tn), lambda j: (0, j)),
        compiler_params=_params(("arbitrary",), 40),
        name="ada",
    )(c_col, w_ada, b_ada)


def _proj_kernel(x_ref, g_ref, sh_ref, sc_ref, w_ref, cb_ref, cos_ref, sin_ref, qg_ref, kg_ref,
                 o_ref, h_scr, *, tn):
    j = pl.program_id(1)

    @pl.when(j == 0)
    def _():
        h_scr[...] = _norm_modulate(x_ref[...], g_ref[...], sh_ref[...], sc_ref[...]).astype(h_scr.dtype)

    acc = _dot(h_scr[...], w_ref[...])

    def head_norm_rope(gain_ref, out_scale):
        for hh in range(tn // HEAD_DIM):
            t = acc[:, hh * HEAD_DIM:(hh + 1) * HEAD_DIM]
            ms = jnp.mean(t * t, axis=-1, keepdims=True)
            y = t * lax.rsqrt(ms + EPS) * gain_ref[...]
            r = y * cos_ref[...] + pltpu.roll(y, HEAD_DIM // 2, axis=1) * sin_ref[...]
            if out_scale != 1.0:
                r = r * out_scale
            o_ref[:, hh * HEAD_DIM:(hh + 1) * HEAD_DIM] = r.astype(o_ref.dtype)

    @pl.when(j < K_COL // tn)
    def _():
        head_norm_rope(qg_ref, HEAD_DIM ** -0.5)

    @pl.when(jnp.logical_and(j >= K_COL // tn, j < V_COL // tn))
    def _():
        head_norm_rope(kg_ref, 1.0)

    @pl.when(jnp.logical_and(j >= V_COL // tn, j < G_COL // tn))
    def _():
        o_ref[...] = acc.astype(o_ref.dtype)

    @pl.when(j >= G_COL // tn)
    def _():
        o_ref[...] = jax.nn.sigmoid(acc + cb_ref[...]).astype(o_ref.dtype)


def _proj(x2, g, sh, sc, w_cat, colbias, cos_t, sin_t, qg, kg):
    s, d = x2.shape
    tm, tn = 1024, 512
    vec = lambda: pl.BlockSpec((1, d), lambda i, j: (0, 0))
    return pl.pallas_call(
        functools.partial(_proj_kernel, tn=tn),
        out_shape=jax.ShapeDtypeStruct((s, PROJ_W), jnp.bfloat16),
        grid=(s // tm, PROJ_W // tn),
        in_specs=[pl.BlockSpec((tm, d), lambda i, j: (i, 0)),
                  vec(), vec(), vec(),
                  pl.BlockSpec((d, tn), lambda i, j: (0, j)),
                  pl.BlockSpec((1, tn), lambda i, j: (0, j)),
                  pl.BlockSpec((tm, HEAD_DIM), lambda i, j: (i, 0)),
                  pl.BlockSpec((tm, HEAD_DIM), lambda i, j: (i, 0)),
                  pl.BlockSpec((1, HEAD_DIM), lambda i, j: (0, 0)),
                  pl.BlockSpec((1, HEAD_DIM), lambda i, j: (0, 0))],
        out_specs=pl.BlockSpec((tm, tn), lambda i, j: (i, j)),
        scratch_shapes=[pltpu.VMEM((tm, d), jnp.bfloat16)],
        compiler_params=_params(("arbitrary", "arbitrary"), 48),
        name="proj",
    )(x2, g, sh, sc, w_cat, colbias, cos_t, sin_t, qg, kg)


def _select_kernel(q_ref, k_ref, o_ref, km_scr, *, tq):
    i = pl.program_id(1)
    nb = k_ref.shape[0] // MOBA_BLOCK

    @pl.when(i == 0)
    def _():
        k = k_ref[...].astype(jnp.float32).reshape(nb, MOBA_BLOCK, HEAD_DIM)
        km = jnp.sum(k, axis=1) * (1.0 / MOBA_BLOCK)
        km = jnp.concatenate([km, jnp.zeros((LANES - nb, HEAD_DIM), jnp.float32)], axis=0)
        p0 = km.astype(jnp.bfloat16)
        r1 = km - p0.astype(jnp.float32)
        p1 = r1.astype(jnp.bfloat16)
        p2 = (r1 - p1.astype(jnp.float32)).astype(jnp.bfloat16)
        km_scr[0] = p0
        km_scr[1] = p1
        km_scr[2] = p2

    q = q_ref[...]
    gate = _dot_nt(km_scr[0], q) + _dot_nt(km_scr[1], q) + _dot_nt(km_scr[2], q)
    blk = lax.broadcasted_iota(jnp.int32, gate.shape, 0)
    qpos = i * tq + lax.broadcasted_iota(jnp.int32, gate.shape, 1)
    past = blk < lax.shift_right_logical(qpos, BLOCK_SHIFT)
    g = jnp.where(past, gate, NEG)
    bias = jnp.full(gate.shape, NEG, jnp.float32)
    for _ in range(MOBA_TOPK):
        top = jnp.max(g, axis=0, keepdims=True)
        first = jnp.min(jnp.where(g == top, blk, LANES), axis=0, keepdims=True)
        pick = blk == first
        bias = jnp.where(jnp.logical_and(pick, past), 0.0, bias)
        g = jnp.where(pick, -jnp.inf, g)
    o_ref[...] = bias.T.astype(o_ref.dtype)


def _select(proj):
    s = proj.shape[0]
    tq = 1024
    kcol = K_COL // HEAD_DIM
    return pl.pallas_call(
        functools.partial(_select_kernel, tq=tq),
        out_shape=jax.ShapeDtypeStruct((N_HEADS, s, LANES), jnp.bfloat16),
        grid=(N_HEADS, s // tq),
        in_specs=[pl.BlockSpec((tq, HEAD_DIM), lambda h, i: (i, h)),
                  pl.BlockSpec((s, HEAD_DIM), lambda h, i: (0, kcol + h))],
        out_specs=pl.BlockSpec((None, tq, LANES), lambda h, i: (h, i, 0)),
        scratch_shapes=[pltpu.VMEM((3, LANES, HEAD_DIM), jnp.bfloat16)],
        compiler_params=_params(("arbitrary", "arbitrary"), 48),
        name="select",
    )(proj, proj)


def _attn_kernel(q_ref, b_ref, k_ref, v_ref, o_ref, kaug_scr):
    i = pl.program_id(1)
    tq = q_ref.shape[0]
    s_len = k_ref.shape[0]

    @pl.when(i == 0)
    def _():
        kaug_scr[:, :HEAD_DIM] = k_ref[...]
        row_blk = lax.shift_right_logical(lax.broadcasted_iota(jnp.int32, (s_len, LANES), 0), BLOCK_SHIFT)
        lane = lax.broadcasted_iota(jnp.int32, (s_len, LANES), 1)
        kaug_scr[:, HEAD_DIM:] = (row_blk == lane).astype(kaug_scr.dtype)

    q = q_ref[...]
    qa = jnp.concatenate([q, b_ref[...]], axis=1)

    d0 = pl.multiple_of(i * MOBA_BLOCK, MOBA_BLOCK)
    s = _dot_nt(q, k_ref[pl.ds(d0, MOBA_BLOCK), :])
    qi = lax.broadcasted_iota(jnp.int32, s.shape, 0)
    ki = lax.broadcasted_iota(jnp.int32, s.shape, 1)
    s = jnp.where(ki <= qi, s, NEG)
    m = jnp.max(s, axis=-1, keepdims=True)
    p = jnp.exp(s - m)
    l = jnp.sum(p, axis=-1, keepdims=True)
    acc = _dot(p.astype(jnp.bfloat16), v_ref[pl.ds(d0, MOBA_BLOCK), :])

    def body(j, carry):
        m, l, acc = carry
        r0 = pl.multiple_of(j * MOBA_BLOCK, MOBA_BLOCK)
        s = _dot_nt(qa, kaug_scr[pl.ds(r0, MOBA_BLOCK), :])
        m_new = jnp.maximum(m, jnp.max(s, axis=-1, keepdims=True))
        alpha = jnp.exp(m - m_new)
        p = jnp.exp(s - m_new)
        l = alpha * l + jnp.sum(p, axis=-1, keepdims=True)
        acc = alpha * acc + _dot(p.astype(jnp.bfloat16), v_ref[pl.ds(r0, MOBA_BLOCK), :])
        return m_new, l, acc

    m, l, acc = lax.fori_loop(0, i, body, (m, l, acc))
    o_ref[...] = (acc / l).astype(o_ref.dtype)


def _attention(proj, bias):
    s = proj.shape[0]
    tq = MOBA_BLOCK
    kcol, vcol = K_COL // HEAD_DIM, V_COL // HEAD_DIM
    return pl.pallas_call(
        _attn_kernel,
        out_shape=jax.ShapeDtypeStruct((s, ATTN_W), jnp.bfloat16),
        grid=(N_HEADS, s // tq),
        in_specs=[pl.BlockSpec((tq, HEAD_DIM), lambda h, i: (i, h)),
                  pl.BlockSpec((None, tq, LANES), lambda h, i: (h, i, 0)),
                  pl.BlockSpec((s, HEAD_DIM), lambda h, i: (0, kcol + h)),
                  pl.BlockSpec((s, HEAD_DIM), lambda h, i: (0, vcol + h))],
        out_specs=pl.BlockSpec((tq, HEAD_DIM), lambda h, i: (i, h)),
        scratch_shapes=[pltpu.VMEM((s, HEAD_DIM + LANES), jnp.bfloat16)],
        compiler_params=_params(("arbitrary", "arbitrary"), 48),
        name="attention",
    )(proj, bias, proj, proj)


POOL_HALO = 16


def _pool_kernel(u_ref, halo_ref, w_ref, ls_ref, o_ref):
    i = pl.program_id(0)
    tm = u_ref.shape[0]
    u = u_ref[...].astype(jnp.float32)
    halo = jnp.where(i == 0, 0.0, halo_ref[...].astype(jnp.float32))
    t = i * tm + lax.broadcasted_iota(jnp.int32, (tm, 1), 0)
    for g, win in enumerate(POOL_WINDOWS):
        cols = slice(g * POOL_GW, (g + 1) * POOL_GW)
        ug = u[:, cols]
        e = jnp.concatenate([halo[:, cols], ug], axis=0)
        span = 1
        while span < win:
            e = e + pltpu.roll(e, span, axis=0)
            span *= 2
        cnt = jnp.minimum(t + 1, win).astype(jnp.float32)
        dg = e[POOL_HALO:, :] / cnt - ug
        mixed = _dot(dg.astype(jnp.bfloat16), w_ref[g])
        o_ref[:, cols] = (mixed * ls_ref[:, cols]).astype(o_ref.dtype)


def _pool(proj, w_grp, ls):
    s = proj.shape[0]
    tm = 1024
    ucol = U_COL // POOL_W
    return pl.pallas_call(
        _pool_kernel,
        out_shape=jax.ShapeDtypeStruct((s, POOL_W), jnp.bfloat16),
        grid=(s // tm,),
        in_specs=[pl.BlockSpec((tm, POOL_W), lambda i: (i, ucol)),
                  pl.BlockSpec((POOL_HALO, POOL_W),
                               lambda i: (jnp.maximum(i * (tm // POOL_HALO) - 1, 0), ucol)),
                  pl.BlockSpec(w_grp.shape, lambda i: (0, 0, 0)),
                  pl.BlockSpec((1, POOL_W), lambda i: (0, 0))],
        out_specs=pl.BlockSpec((tm, POOL_W), lambda i: (i, 0)),
        compiler_params=_params(("arbitrary",), 48),
        name="pool",
    )(proj, proj, w_grp, ls)


def _merge_kernel(a_ref, p_ref, ga_ref, gp_ref, x_ref, g1_ref, wa_ref, wp_ref, wo_ref, o_ref):
    j = pl.program_id(1)

    @pl.when(j == 0)
    def _():
        o_ref[...] = jnp.zeros_like(o_ref)

    ya = _dot(a_ref[...], wa_ref[...])
    yp = _dot(p_ref[...], wp_ref[...])
    merged = ga_ref[...].astype(jnp.float32) * ya + gp_ref[...].astype(jnp.float32) * yp
    o_ref[...] += _dot(merged.astype(jnp.bfloat16), wo_ref[...])

    @pl.when(j == pl.num_programs(1) - 1)
    def _():
        o_ref[...] = x_ref[...] + g1_ref[...] * o_ref[...]


def _merge(attn, pooled, proj, x2, g1, wa, wp, wo):
    s, d = x2.shape
    tm, tn = 512, 512
    ga0, gp0 = G_COL // tn, (G_COL + d) // tn
    return pl.pallas_call(
        _merge_kernel,
        out_shape=jax.ShapeDtypeStruct((s, d), jnp.float32),
        grid=(s // tm, d // tn),
        in_specs=[pl.BlockSpec((tm, ATTN_W), lambda i, j: (i, 0)),
                  pl.BlockSpec((tm, POOL_W), lambda i, j: (i, 0)),
                  pl.BlockSpec((tm, tn), lambda i, j: (i, ga0 + j)),
                  pl.BlockSpec((tm, tn), lambda i, j: (i, gp0 + j)),
                  pl.BlockSpec((tm, d), lambda i, j: (i, 0)),
                  pl.BlockSpec((1, d), lambda i, j: (0, 0)),
                  pl.BlockSpec((ATTN_W, tn), lambda i, j: (0, j)),
                  pl.BlockSpec((POOL_W, tn), lambda i, j: (0, j)),
                  pl.BlockSpec((tn, d), lambda i, j: (j, 0))],
        out_specs=pl.BlockSpec((tm, d), lambda i, j: (i, 0)),
        compiler_params=_params(("arbitrary", "arbitrary"), 48),
        name="merge",
    )(attn, pooled, proj, proj, x2, g1, wa, wp, wo)


def _ffn_kernel(x_ref, g_ref, sh_ref, sc_ref, g2_ref, wua_ref, wub_ref, cwa_ref, cwb_ref, cba_ref, cbb_ref,
                wd_ref, o_ref, h_scr, tail_scr):
    i = pl.program_id(0)
    j = pl.program_id(1)
    tm = x_ref.shape[0]

    @pl.when(j == 0)
    def _():
        h_scr[...] = _norm_modulate(x_ref[...], g_ref[...], sh_ref[...], sc_ref[...]).astype(h_scr.dtype)
        o_ref[...] = jnp.zeros_like(o_ref)

    @pl.when(i == 0)
    def _():
        tail_scr[j] = jnp.zeros(tail_scr.shape[1:], tail_scr.dtype)

    h = h_scr[...]

    def conv(w_ref, cw_ref, cb_ref, half):
        up = _dot(h, w_ref[...])
        ext = jnp.concatenate([tail_scr[j, half], up], axis=0)
        tail_scr[j, half] = up[tm - SUBLANES:, :]
        out = cb_ref[...] + cw_ref[CONV_W - 1:CONV_W, :] * up
        for back in range(1, CONV_W):
            tap = CONV_W - 1 - back
            out = out + cw_ref[tap:tap + 1, :] * pltpu.roll(ext, back, axis=0)[SUBLANES:, :]
        return out

    a = conv(wua_ref, cwa_ref, cba_ref, 0)
    b = conv(wub_ref, cwb_ref, cbb_ref, 1)
    act = (a * jax.nn.sigmoid(a) * b).astype(jnp.bfloat16)
    o_ref[...] += _dot(act, wd_ref[...])

    @pl.when(j == pl.num_programs(1) - 1)
    def _():
        o_ref[...] = x_ref[...] + g2_ref[...] * o_ref[...]


def _ffn(x1, g, sh, sc, g2, w_up, conv_w, conv_b, w_down):
    s, d = x1.shape
    tm, tf = 512, 512
    nf = D_FF // tf
    vec = lambda: pl.BlockSpec((1, d), lambda i, j: (0, 0))
    return pl.pallas_call(
        _ffn_kernel,
        out_shape=jax.ShapeDtypeStruct((s, d), jnp.float32),
        grid=(s // tm, nf),
        in_specs=[pl.BlockSpec((tm, d), lambda i, j: (i, 0)),
                  vec(), vec(), vec(), vec(),
                  pl.BlockSpec((d, tf), lambda i, j: (0, j)),
                  pl.BlockSpec((d, tf), lambda i, j: (0, nf + j)),
                  pl.BlockSpec((CONV_W, tf), lambda i, j: (0, j)),
                  pl.BlockSpec((CONV_W, tf), lambda i, j: (0, nf + j)),
                  pl.BlockSpec((1, tf), lambda i, j: (0, j)),
                  pl.BlockSpec((1, tf), lambda i, j: (0, nf + j)),
                  pl.BlockSpec((tf, d), lambda i, j: (j, 0))],
        out_specs=pl.BlockSpec((tm, d), lambda i, j: (i, 0)),
        scratch_shapes=[pltpu.VMEM((tm, d), jnp.bfloat16),
                        pltpu.VMEM((nf, 2, SUBLANES, tf), jnp.float32)],
        compiler_params=_params(("arbitrary", "arbitrary"), 48),
        name="ffn",
    )(x1, g, sh, sc, g2, w_up, w_up, conv_w, conv_w, conv_b, conv_b, w_down)


def _rope_tables(s):
    half = HEAD_DIM // 2
    inv = ROPE_THETA ** (-jnp.arange(half, dtype=jnp.float32) / half)
    ang = jnp.arange(s).astype(jnp.float32)[:, None] * inv[None, :]
    cos, sin = jnp.cos(ang), jnp.sin(ang)
    return jnp.concatenate([cos, cos], axis=1), jnp.concatenate([-sin, sin], axis=1)


def kernel(x, c, w_ada, b_ada, norm_mix_g, w_in, q_norm_g, k_norm_g, w_pool_grp, pool_scale, w_attn_br,
           w_pool_br, w_gate, b_gate, w_o, norm_ffn_g, w_up, conv_w, conv_b, w_down):
    batch, s, d = x.shape
    assert batch == 1 and d == D_MODEL and w_ada.shape[0] == 1
    bf = jnp.bfloat16
    row = lambda v: v.reshape(1, -1)
    cos_t, sin_t = _rope_tables(s)

    mod = _ada(c.reshape(d, 1), w_ada[0], row(b_ada[0]))
    sh1, sc1, g1, sh2, sc2, g2 = [mod[:, n * d:(n + 1) * d] for n in range(6)]

    w_cat = jnp.concatenate([w_in[0].astype(bf), w_gate[0].astype(bf)], axis=1)
    colbias = jnp.concatenate([jnp.zeros((1, G_COL), jnp.float32), row(b_gate[0])], axis=1)
    x2 = x[0]
    proj = _proj(x2, row(norm_mix_g[0]), sh1, sc1, w_cat, colbias, cos_t, sin_t,
                 row(q_norm_g[0]), row(k_norm_g[0]))
    bias = _select(proj)
    attn = _attention(proj, bias)
    pooled = _pool(proj, w_pool_grp[0].astype(bf), row(pool_scale[0]))
    x1 = _merge(attn, pooled, proj, x2, g1, w_attn_br[0].astype(bf), w_pool_br[0].astype(bf),
                w_o[0].astype(bf))
    out = _ffn(x1, row(norm_ffn_g[0]), sh2, sc2, g2, w_up[0].astype(bf), conv_w[0], row(conv_b[0]),
               w_down[0].astype(bf))
    return out[None]
```

```python
import functools
import math

import jax
import jax.numpy as jnp
from jax import lax
from jax.experimental import pallas as pl
from jax.experimental.pallas import tpu as pltpu

D_MODEL = 2048
N_HEADS = 8
HEAD_DIM = 128
ATTN_W = N_HEADS * HEAD_DIM
MOBA_BLOCK = 256
MOBA_TOPK = 3
POOL_WINDOWS = (2, 4, 8, 16)
POOL_W = 1024
POOL_GW = POOL_W // len(POOL_WINDOWS)
D_FF = 5632
CONV_W = 3
ROPE_THETA = 10000.0
EPS = 1e-6
NEG = -1e30
BLOCK_SHIFT = MOBA_BLOCK.bit_length() - 1
LOG2E = math.log2(math.e)

LANES = 128
SUBLANES = 8
PROJ_W = 4 * ATTN_W + 2 * D_MODEL
Q_COL, K_COL, V_COL, U_COL, G_COL = 0, ATTN_W, 2 * ATTN_W, 3 * ATTN_W, 4 * ATTN_W

MIB = 1024 * 1024


def _dot(a, b):
    return jnp.dot(a, b, preferred_element_type=jnp.float32)


def _dot_nt(a, b):
    return lax.dot_general(a, b, (((1,), (1,)), ((), ())), preferred_element_type=jnp.float32)


def _params(semantics, vmem_mib):
    return pltpu.CompilerParams(dimension_semantics=semantics, vmem_limit_bytes=vmem_mib * MIB)


def _norm_modulate(x, g, shift, scale):
    ms = jnp.mean(x * x, axis=-1, keepdims=True)
    y = x * lax.rsqrt(ms + EPS) * g
    return y * (1.0 + scale) + shift


def _ada_kernel(c_ref, w_ref, b_ref, o_ref):
    c = c_ref[...]
    s = c * jax.nn.sigmoid(c)
    o_ref[...] = jnp.sum(s * w_ref[...], axis=0, keepdims=True) + b_ref[...]


def _ada(c_col, w_ada, b_ada):
    d, n = w_ada.shape
    tn = 1024
    return pl.pallas_call(
        _ada_kernel,
        out_shape=jax.ShapeDtypeStruct((1, n), jnp.float32),
        grid=(n // tn,),
        in_specs=[pl.BlockSpec((d, 1), lambda j: (0, 0)),
                  pl.BlockSpec((d, tn), lambda j: (0, j)),
                  pl.BlockSpec((1, tn), lambda j: (0, j))],
        out_specs=pl.BlockSpec((1, tn), lambda j: (0, j)),
        compiler_params=_params(("arbitrary",), 40),
        name="ada",
    )(c_col, w_ada, b_ada)


def _proj_kernel(x_ref, g_ref, sh_ref, sc_ref, w_ref, cb_ref, cos_ref, sin_ref, qg_ref, kg_ref,
                 o_ref, h_scr, *, tn):
    j = pl.program_id(1)

    @pl.when(j == 0)
    def _():
        h_scr[...] = _norm_modulate(x_ref[...], g_ref[...], sh_ref[...], sc_ref[...]).astype(h_scr.dtype)

    acc = _dot(h_scr[...], w_ref[...])

    def head_norm_rope(gain_ref, out_scale):
        for hh in range(tn // HEAD_DIM):
            t = acc[:, hh * HEAD_DIM:(hh + 1) * HEAD_DIM]
            ms = jnp.mean(t * t, axis=-1, keepdims=True)
            y = t * lax.rsqrt(ms + EPS) * gain_ref[...]
            r = y * cos_ref[...] + pltpu.roll(y, HEAD_DIM // 2, axis=1) * sin_ref[...]
            if out_scale != 1.0:
                r = r * out_scale
            o_ref[:, hh * HEAD_DIM:(hh + 1) * HEAD_DIM] = r.astype(o_ref.dtype)

    @pl.when(j < K_COL // tn)
    def _():
        head_norm_rope(qg_ref, HEAD_DIM ** -0.5 * LOG2E)

    @pl.when(jnp.logical_and(j >= K_COL // tn, j < V_COL // tn))
    def _():
        head_norm_rope(kg_ref, 1.0)

    @pl.when(jnp.logical_and(j >= V_COL // tn, j < G_COL // tn))
    def _():
        o_ref[...] = acc.astype(o_ref.dtype)

    @pl.when(j >= G_COL // tn)
    def _():
        o_ref[...] = jax.nn.sigmoid(acc + cb_ref[...]).astype(o_ref.dtype)


def _proj(x2, g, sh, sc, w_cat, colbias, cos_t, sin_t, qg, kg):
    s, d = x2.shape
    tm, tn = 1024, 512
    vec = lambda: pl.BlockSpec((1, d), lambda i, j: (0, 0))
    return pl.pallas_call(
        functools.partial(_proj_kernel, tn=tn),
        out_shape=jax.ShapeDtypeStruct((s, PROJ_W), jnp.bfloat16),
        grid=(s // tm, PROJ_W // tn),
        in_specs=[pl.BlockSpec((tm, d), lambda i, j: (i, 0)),
                  vec(), vec(), vec(),
                  pl.BlockSpec((d, tn), lambda i, j: (0, j)),
                  pl.BlockSpec((1, tn), lambda i, j: (0, j)),
                  pl.BlockSpec((tm, HEAD_DIM), lambda i, j: (i, 0)),
                  pl.BlockSpec((tm, HEAD_DIM), lambda i, j: (i, 0)),
                  pl.BlockSpec((1, HEAD_DIM), lambda i, j: (0, 0)),
                  pl.BlockSpec((1, HEAD_DIM), lambda i, j: (0, 0))],
        out_specs=pl.BlockSpec((tm, tn), lambda i, j: (i, j)),
        scratch_shapes=[pltpu.VMEM((tm, d), jnp.bfloat16)],
        compiler_params=_params(("arbitrary", "arbitrary"), 48),
        name="proj",
    )(x2, g, sh, sc, w_cat, colbias, cos_t, sin_t, qg, kg)


def _select_kernel(q_ref, k_ref, qt_ref, bias_ref, km_scr, *, tq):
    i = pl.program_id(1)
    nb = k_ref.shape[0] // MOBA_BLOCK

    @pl.when(i == 0)
    def _():
        k = k_ref[...].astype(jnp.float32).reshape(nb, MOBA_BLOCK, HEAD_DIM)
        km = jnp.sum(k, axis=1) * (1.0 / MOBA_BLOCK)
        km = jnp.concatenate([km, jnp.zeros((LANES - nb, HEAD_DIM), jnp.float32)], axis=0)
        p0 = km.astype(jnp.bfloat16)
        r1 = km - p0.astype(jnp.float32)
        p1 = r1.astype(jnp.bfloat16)
        p2 = (r1 - p1.astype(jnp.float32)).astype(jnp.bfloat16)
        km_scr[0] = p0
        km_scr[1] = p1
        km_scr[2] = p2

    q = q_ref[...]
    gate = _dot_nt(km_scr[0], q) + _dot_nt(km_scr[1], q) + _dot_nt(km_scr[2], q)
    blk = lax.broadcasted_iota(jnp.int32, gate.shape, 0)
    qpos = i * tq + lax.broadcasted_iota(jnp.int32, gate.shape, 1)
    own = lax.shift_right_logical(qpos, BLOCK_SHIFT)
    past = blk < own
    g = jnp.where(past, gate, NEG)
    bias = jnp.where(blk == own, 0.0, NEG)
    for _ in range(MOBA_TOPK):
        top = jnp.max(g, axis=0, keepdims=True)
        first = jnp.min(jnp.where(g == top, blk, LANES), axis=0, keepdims=True)
        pick = blk == first
        bias = jnp.where(jnp.logical_and(pick, past), 0.0, bias)
        g = jnp.where(pick, -jnp.inf, g)
    qt_ref[...] = _dot_nt(_eye(HEAD_DIM), q).astype(qt_ref.dtype)
    bias_ref[...] = bias[:nb, :]


def _eye(n):
    return (lax.broadcasted_iota(jnp.int32, (n, n), 0)
            == lax.broadcasted_iota(jnp.int32, (n, n), 1)).astype(jnp.bfloat16)


def _select(proj):
    s = proj.shape[0]
    tq = 1024
    kcol = K_COL // HEAD_DIM
    return pl.pallas_call(
        functools.partial(_select_kernel, tq=tq),
        out_shape=[jax.ShapeDtypeStruct((N_HEADS, HEAD_DIM, s), jnp.bfloat16),
                   jax.ShapeDtypeStruct((N_HEADS, s // MOBA_BLOCK, s), jnp.float32)],
        grid=(N_HEADS, s // tq),
        in_specs=[pl.BlockSpec((tq, HEAD_DIM), lambda h, i: (i, h)),
                  pl.BlockSpec((s, HEAD_DIM), lambda h, i: (0, kcol + h))],
        out_specs=[pl.BlockSpec((None, HEAD_DIM, tq), lambda h, i: (h, 0, i)),
                   pl.BlockSpec((None, s // MOBA_BLOCK, tq), lambda h, i: (h, 0, i))],
        scratch_shapes=[pltpu.VMEM((3, LANES, HEAD_DIM), jnp.bfloat16)],
        compiler_params=_params(("arbitrary", "arbitrary"), 48),
        name="select",
    )(proj, proj)


ATTN_QB = 4
ONES_ROWS = 16
XPOSE_CHUNK = 2048


def _attn_kernel(q_ref, bias_ref, k_ref, v_ref, o_ref, vaug_scr, acc_scr, s_scr, p_scr):
    i = pl.program_id(1)
    s_len = k_ref.shape[0]
    blk = MOBA_BLOCK

    @pl.when(i == 0)
    def _():
        for c in range(s_len // XPOSE_CHUNK):
            keys = slice(c * XPOSE_CHUNK, (c + 1) * XPOSE_CHUNK)
            vaug_scr[:HEAD_DIM, keys] = _dot_nt(_eye(HEAD_DIM), v_ref[keys, :]).astype(vaug_scr.dtype)
        vaug_scr[HEAD_DIM:, :] = jnp.ones((ONES_ROWS, s_len), vaug_scr.dtype)

    ng = ATTN_QB
    n_past = i * ng
    groups = [slice(g * blk, (g + 1) * blk) for g in range(ng)]
    key_le_query = (lax.broadcasted_iota(jnp.int32, (blk, blk), 0)
                    <= lax.broadcasted_iota(jnp.int32, (blk, blk), 1))

    def score_tile(j, slot, g, causal):
        kb = k_ref[pl.ds(pl.multiple_of(j * blk, blk), blk), :]
        s_t = _dot(kb, q_ref[:, groups[g]])
        if causal:
            s_t = jnp.where(key_le_query, s_t, NEG)
        s_scr[slot, :, groups[g]] = s_t
        return jnp.max(s_t, axis=0, keepdims=True)

    def softmax_tile(j, slot, g, m_old, mx):
        b = bias_ref[pl.ds(j, 1), groups[g]]
        m_new = jnp.maximum(m_old, mx + b)
        shift = jnp.maximum(m_new - b, mx)
        p_scr[slot, :, groups[g]] = jnp.exp2(s_scr[slot, :, groups[g]] - shift).astype(p_scr.dtype)
        return m_new, jnp.exp2(m_old - m_new)

    def value_tile(j, slot, g, alpha):
        vb = vaug_scr[:, pl.ds(pl.multiple_of(j * blk, blk), blk)]
        acc_scr[:, groups[g]] = alpha * acc_scr[:, groups[g]] + _dot(vb, p_scr[slot, :, groups[g]])

    def past_step(j, slot, m, mx, a_prev):
        mx_next = tuple(score_tile(j + 1, 1 - slot, g, False) for g in range(ng))
        for g in range(ng):
            value_tile(jnp.maximum(j - 1, 0), 1 - slot, g, a_prev[g])
        m_alpha = [softmax_tile(j, slot, g, m[g], mx[g]) for g in range(ng)]
        return tuple(ma[0] for ma in m_alpha), mx_next, tuple(ma[1] for ma in m_alpha)

    def body(jj, carry):
        return past_step(2 * jj + 1, 1, *past_step(2 * jj, 0, *carry))

    acc_scr[...] = jnp.zeros_like(acc_scr)
    p_scr[1] = jnp.zeros(p_scr.shape[1:], p_scr.dtype)
    init = (tuple(jnp.full((1, blk), NEG, jnp.float32) for _ in range(ng)),
            tuple(score_tile(0, 0, g, False) for g in range(ng)),
            tuple(jnp.ones((1, blk), jnp.float32) for _ in range(ng)))
    m, mx, alpha = lax.fori_loop(0, n_past // 2, body, init)
    m, mx, alpha = list(m), list(mx), list(alpha)

    mx[0] = score_tile(n_past, 0, 0, True)
    for t in range(ng):
        slot = t % 2
        mx_next = {g: score_tile(n_past + t + 1, 1 - slot, g, g == t + 1) for g in range(t + 1, ng)}
        for g in range(max(t - 1, 0), ng):
            value_tile(jnp.maximum(n_past + t - 1, 0), 1 - slot, g, alpha[g])
        for g in range(t, ng):
            m[g], alpha[g] = softmax_tile(n_past + t, slot, g, m[g], mx[g])
        for g, v in mx_next.items():
            mx[g] = v
    value_tile(n_past + ng - 1, (ng - 1) % 2, ng - 1, alpha[ng - 1])
    o_ref[...] = (acc_scr[:HEAD_DIM, :] / acc_scr[HEAD_DIM:HEAD_DIM + 1, :]).T.astype(o_ref.dtype)


def _attention(proj, q_t, bias):
    s = proj.shape[0]
    tq = ATTN_QB * MOBA_BLOCK
    kcol, vcol = K_COL // HEAD_DIM, V_COL // HEAD_DIM
    return pl.pallas_call(
        _attn_kernel,
        out_shape=jax.ShapeDtypeStruct((s, ATTN_W), jnp.bfloat16),
        grid=(N_HEADS, s // tq),
        in_specs=[pl.BlockSpec((None, HEAD_DIM, tq), lambda h, i: (h, 0, i)),
                  pl.BlockSpec((None, s // MOBA_BLOCK, tq), lambda h, i: (h, 0, i)),
                  pl.BlockSpec((s, HEAD_DIM), lambda h, i: (0, kcol + h)),
                  pl.BlockSpec((s, HEAD_DIM), lambda h, i: (0, vcol + h))],
        out_specs=pl.BlockSpec((tq, HEAD_DIM), lambda h, i: (i, h)),
        scratch_shapes=[pltpu.VMEM((HEAD_DIM + ONES_ROWS, s), jnp.bfloat16),
                        pltpu.VMEM((HEAD_DIM + ONES_ROWS, tq), jnp.float32),
                        pltpu.VMEM((2, MOBA_BLOCK, tq), jnp.float32),
                        pltpu.VMEM((2, MOBA_BLOCK, tq), jnp.bfloat16)],
        compiler_params=_params(("arbitrary", "arbitrary"), 56),
        name="attention",
    )(q_t, bias, proj, proj)


POOL_HALO = 16


def _pool_kernel(u_ref, halo_ref, w_ref, ls_ref, o_ref):
    i = pl.program_id(0)
    tm = u_ref.shape[0]
    u = u_ref[...].astype(jnp.float32)
    halo = jnp.where(i == 0, 0.0, halo_ref[...].astype(jnp.float32))
    t = i * tm + lax.broadcasted_iota(jnp.int32, (tm, 1), 0)
    for g, win in enumerate(POOL_WINDOWS):
        cols = slice(g * POOL_GW, (g + 1) * POOL_GW)
        ug = u[:, cols]
        e = jnp.concatenate([halo[:, cols], ug], axis=0)
        span = 1
        while span < win:
            e = e + pltpu.roll(e, span, axis=0)
            span *= 2
        cnt = jnp.minimum(t + 1, win).astype(jnp.float32)
        dg = e[POOL_HALO:, :] / cnt - ug
        mixed = _dot(dg.astype(jnp.bfloat16), w_ref[g])
        o_ref[:, cols] = (mixed * ls_ref[:, cols]).astype(o_ref.dtype)


def _pool(proj, w_grp, ls):
    s = proj.shape[0]
    tm = 1024
    ucol = U_COL // POOL_W
    return pl.pallas_call(
        _pool_kernel,
        out_shape=jax.ShapeDtypeStruct((s, POOL_W), jnp.bfloat16),
        grid=(s // tm,),
        in_specs=[pl.BlockSpec((tm, POOL_W), lambda i: (i, ucol)),
                  pl.BlockSpec((POOL_HALO, POOL_W),
                               lambda i: (jnp.maximum(i * (tm // POOL_HALO) - 1, 0), ucol)),
                  pl.BlockSpec(w_grp.shape, lambda i: (0, 0, 0)),
                  pl.BlockSpec((1, POOL_W), lambda i: (0, 0))],
        out_specs=pl.BlockSpec((tm, POOL_W), lambda i: (i, 0)),
        compiler_params=_params(("arbitrary",), 48),
        name="pool",
    )(proj, proj, w_grp, ls)


def _merge_kernel(a_ref, p_ref, ga_ref, gp_ref, x_ref, g1_ref, wa_ref, wp_ref, wo_ref, o_ref):
    j = pl.program_id(1)

    @pl.when(j == 0)
    def _():
        o_ref[...] = jnp.zeros_like(o_ref)

    ya = _dot(a_ref[...], wa_ref[...])
    yp = _dot(p_ref[...], wp_ref[...])
    merged = ga_ref[...].astype(jnp.float32) * ya + gp_ref[...].astype(jnp.float32) * yp
    o_ref[...] += _dot(merged.astype(jnp.bfloat16), wo_ref[...])

    @pl.when(j == pl.num_programs(1) - 1)
    def _():
        o_ref[...] = x_ref[...] + g1_ref[...] * o_ref[...]


def _merge(attn, pooled, proj, x2, g1, wa, wp, wo):
    s, d = x2.shape
    tm, tn = 512, 512
    ga0, gp0 = G_COL // tn, (G_COL + d) // tn
    return pl.pallas_call(
        _merge_kernel,
        out_shape=jax.ShapeDtypeStruct((s, d), jnp.float32),
        grid=(s // tm, d // tn),
        in_specs=[pl.BlockSpec((tm, ATTN_W), lambda i, j: (i, 0)),
                  pl.BlockSpec((tm, POOL_W), lambda i, j: (i, 0)),
                  pl.BlockSpec((tm, tn), lambda i, j: (i, ga0 + j)),
                  pl.BlockSpec((tm, tn), lambda i, j: (i, gp0 + j)),
                  pl.BlockSpec((tm, d), lambda i, j: (i, 0)),
                  pl.BlockSpec((1, d), lambda i, j: (0, 0)),
                  pl.BlockSpec((ATTN_W, tn), lambda i, j: (0, j)),
                  pl.BlockSpec((POOL_W, tn), lambda i, j: (0, j)),
                  pl.BlockSpec((tn, d), lambda i, j: (j, 0))],
        out_specs=pl.BlockSpec((tm, d), lambda i, j: (i, 0)),
        compiler_params=_params(("arbitrary", "arbitrary"), 48),
        name="merge",
    )(attn, pooled, proj, proj, x2, g1, wa, wp, wo)


def _ffn_kernel(x_ref, g_ref, sh_ref, sc_ref, g2_ref, wua_ref, wub_ref, cwa_ref, cwb_ref, cba_ref, cbb_ref,
                wd_ref, o_ref, h_scr, tail_scr):
    i = pl.program_id(0)
    j = pl.program_id(1)
    tm = x_ref.shape[0]

    @pl.when(j == 0)
    def _():
        h_scr[...] = _norm_modulate(x_ref[...], g_ref[...], sh_ref[...], sc_ref[...]).astype(h_scr.dtype)
        o_ref[...] = jnp.zeros_like(o_ref)

    @pl.when(i == 0)
    def _():
        tail_scr[j] = jnp.zeros(tail_scr.shape[1:], tail_scr.dtype)

    h = h_scr[...]

    def conv(w_ref, cw_ref, cb_ref, half):
        up = _dot(h, w_ref[...])
        ext = jnp.concatenate([tail_scr[j, half], up], axis=0)
        tail_scr[j, half] = up[tm - SUBLANES:, :]
        out = cb_ref[...] + cw_ref[CONV_W - 1:CONV_W, :] * up
        for back in range(1, CONV_W):
            tap = CONV_W - 1 - back
            out = out + cw_ref[tap:tap + 1, :] * pltpu.roll(ext, back, axis=0)[SUBLANES:, :]
        return out

    a = conv(wua_ref, cwa_ref, cba_ref, 0)
    b = conv(wub_ref, cwb_ref, cbb_ref, 1)
    act = (a * jax.nn.sigmoid(a) * b).astype(jnp.bfloat16)
    o_ref[...] += _dot(act, wd_ref[...])

    @pl.when(j == pl.num_programs(1) - 1)
    def _():
        o_ref[...] = x_ref[...] + g2_ref[...] * o_ref[...]


def _ffn(x1, g, sh, sc, g2, w_up, conv_w, conv_b, w_down):
    s, d = x1.shape
    tm, tf = 512, 512
    nf = D_FF // tf
    vec = lambda: pl.BlockSpec((1, d), lambda i, j: (0, 0))
    return pl.pallas_call(
        _ffn_kernel,
        out_shape=jax.ShapeDtypeStruct((s, d), jnp.float32),
        grid=(s // tm, nf),
        in_specs=[pl.BlockSpec((tm, d), lambda i, j: (i, 0)),
                  vec(), vec(), vec(), vec(),
                  pl.BlockSpec((d, tf), lambda i, j: (0, j)),
                  pl.BlockSpec((d, tf), lambda i, j: (0, nf + j)),
                  pl.BlockSpec((CONV_W, tf), lambda i, j: (0, j)),
                  pl.BlockSpec((CONV_W, tf), lambda i, j: (0, nf + j)),
                  pl.BlockSpec((1, tf), lambda i, j: (0, j)),
                  pl.BlockSpec((1, tf), lambda i, j: (0, nf + j)),
                  pl.BlockSpec((tf, d), lambda i, j: (j, 0))],
        out_specs=pl.BlockSpec((tm, d), lambda i, j: (i, 0)),
        scratch_shapes=[pltpu.VMEM((tm, d), jnp.bfloat16),
                        pltpu.VMEM((nf, 2, SUBLANES, tf), jnp.float32)],
        compiler_params=_params(("arbitrary", "arbitrary"), 48),
        name="ffn",
    )(x1, g, sh, sc, g2, w_up, w_up, conv_w, conv_w, conv_b, conv_b, w_down)


def _rope_tables(s):
    half = HEAD_DIM // 2
    inv = ROPE_THETA ** (-jnp.arange(half, dtype=jnp.float32) / half)
    ang = jnp.arange(s).astype(jnp.float32)[:, None] * inv[None, :]
    cos, sin = jnp.cos(ang), jnp.sin(ang)
    return jnp.concatenate([cos, cos], axis=1), jnp.concatenate([-sin, sin], axis=1)


def kernel(x, c, w_ada, b_ada, norm_mix_g, w_in, q_norm_g, k_norm_g, w_pool_grp, pool_scale, w_attn_br,
           w_pool_br, w_gate, b_gate, w_o, norm_ffn_g, w_up, conv_w, conv_b, w_down):
    batch, s, d = x.shape
    assert batch == 1 and d == D_MODEL and w_ada.shape[0] == 1
    bf = jnp.bfloat16
    row = lambda v: v.reshape(1, -1)
    cos_t, sin_t = _rope_tables(s)

    mod = _ada(c.reshape(d, 1), w_ada[0], row(b_ada[0]))
    sh1, sc1, g1, sh2, sc2, g2 = [mod[:, n * d:(n + 1) * d] for n in range(6)]

    w_cat = jnp.concatenate([w_in[0].astype(bf), w_gate[0].astype(bf)], axis=1)
    colbias = jnp.concatenate([jnp.zeros((1, G_COL), jnp.float32), row(b_gate[0])], axis=1)
    x2 = x[0]
    proj = _proj(x2, row(norm_mix_g[0]), sh1, sc1, w_cat, colbias, cos_t, sin_t,
                 row(q_norm_g[0]), row(k_norm_g[0]))
    q_t, bias = _select(proj)
    attn = _attention(proj, q_t, bias)
    pooled = _pool(proj, w_pool_grp[0].astype(bf), row(pool_scale[0]))
    x1 = _merge(attn, pooled, proj, x2, g1, w_attn_br[0].astype(bf), w_pool_br[0].astype(bf),
                w_o[0].astype(bf))
    out = _ffn(x1, row(norm_ffn_g[0]), sh2, sc2, g2, w_up[0].astype(bf), conv_w[0], row(conv_b[0]),
               w_down[0].astype(bf))
    return out[None]
```

```python
import functools
import math

import jax
import jax.numpy as jnp
from jax import lax
from jax.experimental import pallas as pl
from jax.experimental.pallas import tpu as pltpu

D_MODEL = 2048
N_HEADS = 8
HEAD_DIM = 128
ATTN_W = N_HEADS * HEAD_DIM
MOBA_BLOCK = 256
MOBA_TOPK = 3
POOL_WINDOWS = (2, 4, 8, 16)
POOL_W = 1024
POOL_GW = POOL_W // len(POOL_WINDOWS)
D_FF = 5632
CONV_W = 3
ROPE_THETA = 10000.0
EPS = 1e-6
NEG = -1e30
BLOCK_SHIFT = MOBA_BLOCK.bit_length() - 1
LOG2E = math.log2(math.e)

LANES = 128
SUBLANES = 8
PROJ_W = 4 * ATTN_W + 2 * D_MODEL
Q_COL, K_COL, V_COL, U_COL, G_COL = 0, ATTN_W, 2 * ATTN_W, 3 * ATTN_W, 4 * ATTN_W

MIB = 1024 * 1024


def _dot(a, b):
    return jnp.dot(a, b, preferred_element_type=jnp.float32)


def _dot_nt(a, b):
    return lax.dot_general(a, b, (((1,), (1,)), ((), ())), preferred_element_type=jnp.float32)


def _params(semantics, vmem_mib):
    return pltpu.CompilerParams(dimension_semantics=semantics, vmem_limit_bytes=vmem_mib * MIB)


def _norm_modulate(x, g, shift, scale):
    ms = jnp.mean(x * x, axis=-1, keepdims=True)
    y = x * lax.rsqrt(ms + EPS) * g
    return y * (1.0 + scale) + shift


def _ada_kernel(c_ref, w_ref, b_ref, o_ref):
    c = c_ref[...]
    s = c * jax.nn.sigmoid(c)
    o_ref[...] = jnp.sum(s * w_ref[...], axis=0, keepdims=True) + b_ref[...]


def _ada(c_col, w_ada, b_ada):
    d, n = w_ada.shape
    tn = 1024
    return pl.pallas_call(
        _ada_kernel,
        out_shape=jax.ShapeDtypeStruct((1, n), jnp.float32),
        grid=(n // tn,),
        in_specs=[pl.BlockSpec((d, 1), lambda j: (0, 0)),
                  pl.BlockSpec((d, tn), lambda j: (0, j)),
                  pl.BlockSpec((1, tn), lambda j: (0, j))],
        out_specs=pl.BlockSpec((1, tn), lambda j: (0, j)),
        compiler_params=_params(("arbitrary",), 40),
        name="ada",
    )(c_col, w_ada, b_ada)


def _proj_kernel(x_ref, g_ref, sh_ref, sc_ref, w_ref, cb_ref, cos_ref, sin_ref, qg_ref, kg_ref,
                 o_ref, h_scr, *, tn):
    j = pl.program_id(1)

    @pl.when(j == 0)
    def _():
        h_scr[...] = _norm_modulate(x_ref[...], g_ref[...], sh_ref[...], sc_ref[...]).astype(h_scr.dtype)

    acc = _dot(h_scr[...], w_ref[...])

    def head_norm_rope(gain_ref, out_scale):
        for hh in range(tn // HEAD_DIM):
            t = acc[:, hh * HEAD_DIM:(hh + 1) * HEAD_DIM]
            ms = jnp.mean(t * t, axis=-1, keepdims=True)
            y = t * lax.rsqrt(ms + EPS) * gain_ref[...]
            r = y * cos_ref[...] + pltpu.roll(y, HEAD_DIM // 2, axis=1) * sin_ref[...]
            if out_scale != 1.0:
                r = r * out_scale
            o_ref[:, hh * HEAD_DIM:(hh + 1) * HEAD_DIM] = r.astype(o_ref.dtype)

    @pl.when(j < K_COL // tn)
    def _():
        head_norm_rope(qg_ref, HEAD_DIM ** -0.5 * LOG2E)

    @pl.when(jnp.logical_and(j >= K_COL // tn, j < V_COL // tn))
    def _():
        head_norm_rope(kg_ref, 1.0)

    @pl.when(jnp.logical_and(j >= V_COL // tn, j < G_COL // tn))
    def _():
        o_ref[...] = acc.astype(o_ref.dtype)

    @pl.when(j >= G_COL // tn)
    def _():
        o_ref[...] = jax.nn.sigmoid(acc + cb_ref[...]).astype(o_ref.dtype)


def _proj(x2, g, sh, sc, w_cat, colbias, cos_t, sin_t, qg, kg):
    s, d = x2.shape
    tm, tn = 1024, 512
    vec = lambda: pl.BlockSpec((1, d), lambda i, j: (0, 0))
    return pl.pallas_call(
        functools.partial(_proj_kernel, tn=tn),
        out_shape=jax.ShapeDtypeStruct((s, PROJ_W), jnp.bfloat16),
        grid=(s // tm, PROJ_W // tn),
        in_specs=[pl.BlockSpec((tm, d), lambda i, j: (i, 0)),
                  vec(), vec(), vec(),
                  pl.BlockSpec((d, tn), lambda i, j: (0, j)),
                  pl.BlockSpec((1, tn), lambda i, j: (0, j)),
                  pl.BlockSpec((tm, HEAD_DIM), lambda i, j: (i, 0)),
                  pl.BlockSpec((tm, HEAD_DIM), lambda i, j: (i, 0)),
                  pl.BlockSpec((1, HEAD_DIM), lambda i, j: (0, 0)),
                  pl.BlockSpec((1, HEAD_DIM), lambda i, j: (0, 0))],
        out_specs=pl.BlockSpec((tm, tn), lambda i, j: (i, j)),
        scratch_shapes=[pltpu.VMEM((tm, d), jnp.bfloat16)],
        compiler_params=_params(("arbitrary", "arbitrary"), 48),
        name="proj",
    )(x2, g, sh, sc, w_cat, colbias, cos_t, sin_t, qg, kg)


def _select_kernel(q_ref, k_ref, qt_ref, bias_ref, km_scr, *, tq):
    i = pl.program_id(1)
    nb = k_ref.shape[0] // MOBA_BLOCK

    @pl.when(i == 0)
    def _():
        k = k_ref[...].astype(jnp.float32).reshape(nb, MOBA_BLOCK, HEAD_DIM)
        km = jnp.sum(k, axis=1) * (1.0 / MOBA_BLOCK)
        km = jnp.concatenate([km, jnp.zeros((LANES - nb, HEAD_DIM), jnp.float32)], axis=0)
        p0 = km.astype(jnp.bfloat16)
        r1 = km - p0.astype(jnp.float32)
        p1 = r1.astype(jnp.bfloat16)
        p2 = (r1 - p1.astype(jnp.float32)).astype(jnp.bfloat16)
        km_scr[0] = p0
        km_scr[1] = p1
        km_scr[2] = p2

    q = q_ref[...]
    gate = _dot_nt(km_scr[0], q) + _dot_nt(km_scr[1], q) + _dot_nt(km_scr[2], q)
    blk = lax.broadcasted_iota(jnp.int32, gate.shape, 0)
    qpos = i * tq + lax.broadcasted_iota(jnp.int32, gate.shape, 1)
    own = lax.shift_right_logical(qpos, BLOCK_SHIFT)
    past = blk < own
    g = jnp.where(past, gate, NEG)
    bias = jnp.where(blk == own, 0.0, NEG)
    for _ in range(MOBA_TOPK):
        top = jnp.max(g, axis=0, keepdims=True)
        first = jnp.min(jnp.where(g == top, blk, LANES), axis=0, keepdims=True)
        pick = blk == first
        bias = jnp.where(jnp.logical_and(pick, past), 0.0, bias)
        g = jnp.where(pick, -jnp.inf, g)
    qt_ref[...] = _dot_nt(_eye(HEAD_DIM), q).astype(qt_ref.dtype)
    bias_ref[...] = bias[:nb, :]


def _eye(n):
    return (lax.broadcasted_iota(jnp.int32, (n, n), 0)
            == lax.broadcasted_iota(jnp.int32, (n, n), 1)).astype(jnp.bfloat16)


def _select(proj):
    s = proj.shape[0]
    tq = 1024
    kcol = K_COL // HEAD_DIM
    return pl.pallas_call(
        functools.partial(_select_kernel, tq=tq),
        out_shape=[jax.ShapeDtypeStruct((N_HEADS, HEAD_DIM, s), jnp.bfloat16),
                   jax.ShapeDtypeStruct((N_HEADS, s // MOBA_BLOCK, s), jnp.float32)],
        grid=(N_HEADS, s // tq),
        in_specs=[pl.BlockSpec((tq, HEAD_DIM), lambda h, i: (i, h)),
                  pl.BlockSpec((s, HEAD_DIM), lambda h, i: (0, kcol + h))],
        out_specs=[pl.BlockSpec((None, HEAD_DIM, tq), lambda h, i: (h, 0, i)),
                   pl.BlockSpec((None, s // MOBA_BLOCK, tq), lambda h, i: (h, 0, i))],
        scratch_shapes=[pltpu.VMEM((3, LANES, HEAD_DIM), jnp.bfloat16)],
        compiler_params=_params(("arbitrary", "arbitrary"), 48),
        name="select",
    )(proj, proj)


ATTN_QB = 4
ATTN_UNROLL = 4
ONES_ROWS = 16
XPOSE_CHUNK = 2048


def _attn_kernel(q_ref, bias_ref, k_ref, v_ref, o_ref, vaug_scr, acc_scr, s_scr, p_scr):
    i = pl.program_id(1)
    s_len = k_ref.shape[0]
    blk = MOBA_BLOCK

    @pl.when(i == 0)
    def _():
        for c in range(s_len // XPOSE_CHUNK):
            keys = slice(c * XPOSE_CHUNK, (c + 1) * XPOSE_CHUNK)
            vaug_scr[:HEAD_DIM, keys] = _dot_nt(_eye(HEAD_DIM), v_ref[keys, :]).astype(vaug_scr.dtype)
        vaug_scr[HEAD_DIM:, :] = jnp.ones((ONES_ROWS, s_len), vaug_scr.dtype)

    ng = ATTN_QB
    n_past = i * ng
    groups = [slice(g * blk, (g + 1) * blk) for g in range(ng)]
    key_le_query = (lax.broadcasted_iota(jnp.int32, (blk, blk), 0)
                    <= lax.broadcasted_iota(jnp.int32, (blk, blk), 1))

    def score_tile(j, slot, g, causal):
        kb = k_ref[pl.ds(pl.multiple_of(j * blk, blk), blk), :]
        s_t = _dot(kb, q_ref[:, groups[g]])
        if causal:
            s_t = jnp.where(key_le_query, s_t, NEG)
        s_scr[slot, :, groups[g]] = s_t
        return jnp.max(s_t, axis=0, keepdims=True)

    def softmax_tile(j, slot, g, m_old, mx):
        b = bias_ref[pl.ds(j, 1), groups[g]]
        m_new = jnp.maximum(m_old, mx + b)
        shift = jnp.maximum(m_new - b, mx)
        p_scr[slot, :, groups[g]] = jnp.exp2(s_scr[slot, :, groups[g]] - shift).astype(p_scr.dtype)
        return m_new, jnp.exp2(m_old - m_new)

    def value_tile(j, slot, g, alpha):
        vb = vaug_scr[:, pl.ds(pl.multiple_of(j * blk, blk), blk)]
        acc_scr[:, groups[g]] = alpha * acc_scr[:, groups[g]] + _dot(vb, p_scr[slot, :, groups[g]])

    def past_step(j, slot, m, mx, a_prev):
        mx_next = tuple(score_tile(j + 1, 1 - slot, g, False) for g in range(ng))
        for g in range(ng):
            value_tile(jnp.maximum(j - 1, 0), 1 - slot, g, a_prev[g])
        m_alpha = [softmax_tile(j, slot, g, m[g], mx[g]) for g in range(ng)]
        return tuple(ma[0] for ma in m_alpha), mx_next, tuple(ma[1] for ma in m_alpha)

    def body(jj, carry):
        for u in range(ATTN_UNROLL):
            carry = past_step(ATTN_UNROLL * jj + u, u % 2, *carry)
        return carry

    acc_scr[...] = jnp.zeros_like(acc_scr)
    p_scr[1] = jnp.zeros(p_scr.shape[1:], p_scr.dtype)
    init = (tuple(jnp.full((1, blk), NEG, jnp.float32) for _ in range(ng)),
            tuple(score_tile(0, 0, g, False) for g in range(ng)),
            tuple(jnp.ones((1, blk), jnp.float32) for _ in range(ng)))
    m, mx, alpha = lax.fori_loop(0, n_past // ATTN_UNROLL, body, init)
    m, mx, alpha = list(m), list(mx), list(alpha)

    mx[0] = score_tile(n_past, 0, 0, True)
    for t in range(ng):
        slot = t % 2
        mx_next = {g: score_tile(n_past + t + 1, 1 - slot, g, g == t + 1) for g in range(t + 1, ng)}
        for g in range(max(t - 1, 0), ng):
            value_tile(jnp.maximum(n_past + t - 1, 0), 1 - slot, g, alpha[g])
        for g in range(t, ng):
            m[g], alpha[g] = softmax_tile(n_past + t, slot, g, m[g], mx[g])
        for g, v in mx_next.items():
            mx[g] = v
    value_tile(n_past + ng - 1, (ng - 1) % 2, ng - 1, alpha[ng - 1])
    o_ref[...] = (acc_scr[:HEAD_DIM, :] / acc_scr[HEAD_DIM:HEAD_DIM + 1, :]).T.astype(o_ref.dtype)


def _attention(proj, q_t, bias):
    s = proj.shape[0]
    tq = ATTN_QB * MOBA_BLOCK
    kcol, vcol = K_COL // HEAD_DIM, V_COL // HEAD_DIM
    return pl.pallas_call(
        _attn_kernel,
        out_shape=jax.ShapeDtypeStruct((s, ATTN_W), jnp.bfloat16),
        grid=(N_HEADS, s // tq),
        in_specs=[pl.BlockSpec((None, HEAD_DIM, tq), lambda h, i: (h, 0, i)),
                  pl.BlockSpec((None, s // MOBA_BLOCK, tq), lambda h, i: (h, 0, i)),
                  pl.BlockSpec((s, HEAD_DIM), lambda h, i: (0, kcol + h)),
                  pl.BlockSpec((s, HEAD_DIM), lambda h, i: (0, vcol + h))],
        out_specs=pl.BlockSpec((tq, HEAD_DIM), lambda h, i: (i, h)),
        scratch_shapes=[pltpu.VMEM((HEAD_DIM + ONES_ROWS, s), jnp.bfloat16),
                        pltpu.VMEM((HEAD_DIM + ONES_ROWS, tq), jnp.float32),
                        pltpu.VMEM((2, MOBA_BLOCK, tq), jnp.float32),
                        pltpu.VMEM((2, MOBA_BLOCK, tq), jnp.bfloat16)],
        compiler_params=_params(("arbitrary", "arbitrary"), 56),
        name="attention",
    )(q_t, bias, proj, proj)


POOL_HALO = 16


def _pool_kernel(u_ref, halo_ref, w_ref, ls_ref, o_ref):
    i = pl.program_id(0)
    tm = u_ref.shape[0]
    u = u_ref[...].astype(jnp.float32)
    halo = jnp.where(i == 0, 0.0, halo_ref[...].astype(jnp.float32))
    t = i * tm + lax.broadcasted_iota(jnp.int32, (tm, 1), 0)
    for g, win in enumerate(POOL_WINDOWS):
        cols = slice(g * POOL_GW, (g + 1) * POOL_GW)
        ug = u[:, cols]
        e = jnp.concatenate([halo[:, cols], ug], axis=0)
        span = 1
        while span < win:
            e = e + pltpu.roll(e, span, axis=0)
            span *= 2
        cnt = jnp.minimum(t + 1, win).astype(jnp.float32)
        dg = e[POOL_HALO:, :] / cnt - ug
        mixed = _dot(dg.astype(jnp.bfloat16), w_ref[g])
        o_ref[:, cols] = (mixed * ls_ref[:, cols]).astype(o_ref.dtype)


def _pool(proj, w_grp, ls):
    s = proj.shape[0]
    tm = 1024
    ucol = U_COL // POOL_W
    return pl.pallas_call(
        _pool_kernel,
        out_shape=jax.ShapeDtypeStruct((s, POOL_W), jnp.bfloat16),
        grid=(s // tm,),
        in_specs=[pl.BlockSpec((tm, POOL_W), lambda i: (i, ucol)),
                  pl.BlockSpec((POOL_HALO, POOL_W),
                               lambda i: (jnp.maximum(i * (tm // POOL_HALO) - 1, 0), ucol)),
                  pl.BlockSpec(w_grp.shape, lambda i: (0, 0, 0)),
                  pl.BlockSpec((1, POOL_W), lambda i: (0, 0))],
        out_specs=pl.BlockSpec((tm, POOL_W), lambda i: (i, 0)),
        compiler_params=_params(("arbitrary",), 48),
        name="pool",
    )(proj, proj, w_grp, ls)


def _merge_kernel(a_ref, p_ref, ga_ref, gp_ref, x_ref, g1_ref, wa_ref, wp_ref, wo_ref, o_ref):
    j = pl.program_id(1)

    @pl.when(j == 0)
    def _():
        o_ref[...] = jnp.zeros_like(o_ref)

    ya = _dot(a_ref[...], wa_ref[...])
    yp = _dot(p_ref[...], wp_ref[...])
    merged = ga_ref[...].astype(jnp.float32) * ya + gp_ref[...].astype(jnp.float32) * yp
    o_ref[...] += _dot(merged.astype(jnp.bfloat16), wo_ref[...])

    @pl.when(j == pl.num_programs(1) - 1)
    def _():
        o_ref[...] = x_ref[...] + g1_ref[...] * o_ref[...]


def _merge(attn, pooled, proj, x2, g1, wa, wp, wo):
    s, d = x2.shape
    tm, tn = 512, 512
    ga0, gp0 = G_COL // tn, (G_COL + d) // tn
    return pl.pallas_call(
        _merge_kernel,
        out_shape=jax.ShapeDtypeStruct((s, d), jnp.float32),
        grid=(s // tm, d // tn),
        in_specs=[pl.BlockSpec((tm, ATTN_W), lambda i, j: (i, 0)),
                  pl.BlockSpec((tm, POOL_W), lambda i, j: (i, 0)),
                  pl.BlockSpec((tm, tn), lambda i, j: (i, ga0 + j)),
                  pl.BlockSpec((tm, tn), lambda i, j: (i, gp0 + j)),
                  pl.BlockSpec((tm, d), lambda i, j: (i, 0)),
                  pl.BlockSpec((1, d), lambda i, j: (0, 0)),
                  pl.BlockSpec((ATTN_W, tn), lambda i, j: (0, j)),
                  pl.BlockSpec((POOL_W, tn), lambda i, j: (0, j)),
                  pl.BlockSpec((tn, d), lambda i, j: (j, 0))],
        out_specs=pl.BlockSpec((tm, d), lambda i, j: (i, 0)),
        compiler_params=_params(("arbitrary", "arbitrary"), 48),
        name="merge",
    )(attn, pooled, proj, proj, x2, g1, wa, wp, wo)


def _ffn_kernel(x_ref, g_ref, sh_ref, sc_ref, g2_ref, wua_ref, wub_ref, cwa_ref, cwb_ref, cba_ref, cbb_ref,
                wd_ref, o_ref, h_scr, tail_scr):
    i = pl.program_id(0)
    j = pl.program_id(1)
    tm = x_ref.shape[0]

    @pl.when(j == 0)
    def _():
        h_scr[...] = _norm_modulate(x_ref[...], g_ref[...], sh_ref[...], sc_ref[...]).astype(h_scr.dtype)
        o_ref[...] = jnp.zeros_like(o_ref)

    @pl.when(i == 0)
    def _():
        tail_scr[j] = jnp.zeros(tail_scr.shape[1:], tail_scr.dtype)

    h = h_scr[...]

    def conv(w_ref, cw_ref, cb_ref, half):
        up = _dot(h, w_ref[...])
        ext = jnp.concatenate([tail_scr[j, half], up], axis=0)
        tail_scr[j, half] = up[tm - SUBLANES:, :]
        out = cb_ref[...] + cw_ref[CONV_W - 1:CONV_W, :] * up
        for back in range(1, CONV_W):
            tap = CONV_W - 1 - back
            out = out + cw_ref[tap:tap + 1, :] * pltpu.roll(ext, back, axis=0)[SUBLANES:, :]
        return out

    a = conv(wua_ref, cwa_ref, cba_ref, 0)
    b = conv(wub_ref, cwb_ref, cbb_ref, 1)
    act = (a * jax.nn.sigmoid(a) * b).astype(jnp.bfloat16)
    o_ref[...] += _dot(act, wd_ref[...])

    @pl.when(j == pl.num_programs(1) - 1)
    def _():
        o_ref[...] = x_ref[...] + g2_ref[...] * o_ref[...]


def _ffn(x1, g, sh, sc, g2, w_up, conv_w, conv_b, w_down):
    s, d = x1.shape
    tm, tf = 512, 512
    nf = D_FF // tf
    vec = lambda: pl.BlockSpec((1, d), lambda i, j: (0, 0))
    return pl.pallas_call(
        _ffn_kernel,
        out_shape=jax.ShapeDtypeStruct((s, d), jnp.float32),
        grid=(s // tm, nf),
        in_specs=[pl.BlockSpec((tm, d), lambda i, j: (i, 0)),
                  vec(), vec(), vec(), vec(),
                  pl.BlockSpec((d, tf), lambda i, j: (0, j)),
                  pl.BlockSpec((d, tf), lambda i, j: (0, nf + j)),
                  pl.BlockSpec((CONV_W, tf), lambda i, j: (0, j)),
                  pl.BlockSpec((CONV_W, tf), lambda i, j: (0, nf + j)),
                  pl.BlockSpec((1, tf), lambda i, j: (0, j)),
                  pl.BlockSpec((1, tf), lambda i, j: (0, nf + j)),
                  pl.BlockSpec((tf, d), lambda i, j: (j, 0))],
        out_specs=pl.BlockSpec((tm, d), lambda i, j: (i, 0)),
        scratch_shapes=[pltpu.VMEM((tm, d), jnp.bfloat16),
                        pltpu.VMEM((nf, 2, SUBLANES, tf), jnp.float32)],
        compiler_params=_params(("arbitrary", "arbitrary"), 48),
        name="ffn",
    )(x1, g, sh, sc, g2, w_up, w_up, conv_w, conv_w, conv_b, conv_b, w_down)


def _rope_tables(s):
    half = HEAD_DIM // 2
    inv = ROPE_THETA ** (-jnp.arange(half, dtype=jnp.float32) / half)
    ang = jnp.arange(s).astype(jnp.float32)[:, None] * inv[None, :]
    cos, sin = jnp.cos(ang), jnp.sin(ang)
    return jnp.concatenate([cos, cos], axis=1), jnp.concatenate([-sin, sin], axis=1)


def kernel(x, c, w_ada, b_ada, norm_mix_g, w_in, q_norm_g, k_norm_g, w_pool_grp, pool_scale, w_attn_br,
           w_pool_br, w_gate, b_gate, w_o, norm_ffn_g, w_up, conv_w, conv_b, w_down):
    batch, s, d = x.shape
    assert batch == 1 and d == D_MODEL and w_ada.shape[0] == 1
    bf = jnp.bfloat16
    row = lambda v: v.reshape(1, -1)
    cos_t, sin_t = _rope_tables(s)

    mod = _ada(c.reshape(d, 1), w_ada[0], row(b_ada[0]))
    sh1, sc1, g1, sh2, sc2, g2 = [mod[:, n * d:(n + 1) * d] for n in range(6)]

    w_cat = jnp.concatenate([w_in[0], w_gate[0]], axis=1).astype(bf)
    colbias = jnp.concatenate([jnp.zeros((1, G_COL), jnp.float32), row(b_gate[0])], axis=1)
    x2 = x[0]
    proj = _proj(x2, row(norm_mix_g[0]), sh1, sc1, w_cat, colbias, cos_t, sin_t,
                 row(q_norm_g[0]), row(k_norm_g[0]))
    q_t, bias = _select(proj)
    attn = _attention(proj, q_t, bias)
    pooled = _pool(proj, w_pool_grp[0].astype(bf), row(pool_scale[0]))
    x1 = _merge(attn, pooled, proj, x2, g1, w_attn_br[0].astype(bf), w_pool_br[0].astype(bf),
                w_o[0].astype(bf))
    out = _ffn(x1, row(norm_ffn_g[0]), sh2, sc2, g2, w_up[0].astype(bf), conv_w[0], row(conv_b[0]),
               w_down[0].astype(bf))
    return out[None]
```

```python
import functools
import math

import jax
import jax.numpy as jnp
from jax import lax
from jax.experimental import pallas as pl
from jax.experimental.pallas import tpu as pltpu

D_MODEL = 2048
N_HEADS = 8
HEAD_DIM = 128
ATTN_W = N_HEADS * HEAD_DIM
MOBA_BLOCK = 256
MOBA_TOPK = 3
POOL_WINDOWS = (2, 4, 8, 16)
POOL_W = 1024
POOL_GW = POOL_W // len(POOL_WINDOWS)
D_FF = 5632
CONV_W = 3
ROPE_THETA = 10000.0
EPS = 1e-6
NEG = -1e30
BLOCK_SHIFT = MOBA_BLOCK.bit_length() - 1
LOG2E = math.log2(math.e)

LANES = 128
SUBLANES = 8
PROJ_W = 4 * ATTN_W + 2 * D_MODEL
Q_COL, K_COL, V_COL, U_COL, G_COL = 0, ATTN_W, 2 * ATTN_W, 3 * ATTN_W, 4 * ATTN_W

MIB = 1024 * 1024


def _dot(a, b):
    return jnp.dot(a, b, preferred_element_type=jnp.float32)


def _dot_nt(a, b):
    return lax.dot_general(a, b, (((1,), (1,)), ((), ())), preferred_element_type=jnp.float32)


def _sigmoid(x):
    return 0.5 * jnp.tanh(0.5 * x) + 0.5


def _params(semantics, vmem_mib):
    return pltpu.CompilerParams(dimension_semantics=semantics, vmem_limit_bytes=vmem_mib * MIB)


def _norm_modulate(x, g, shift, scale):
    ms = jnp.mean(x * x, axis=-1, keepdims=True)
    y = x * lax.rsqrt(ms + EPS) * g
    return y * (1.0 + scale) + shift


def _ada_kernel(c_ref, w_ref, b_ref, o_ref):
    c = c_ref[...]
    s = c * jax.nn.sigmoid(c)
    o_ref[...] = jnp.sum(s * w_ref[...], axis=0, keepdims=True) + b_ref[...]


def _ada(c_col, w_ada, b_ada):
    d, n = w_ada.shape
    tn = 1024
    return pl.pallas_call(
        _ada_kernel,
        out_shape=jax.ShapeDtypeStruct((1, n), jnp.float32),
        grid=(n // tn,),
        in_specs=[pl.BlockSpec((d, 1), lambda j: (0, 0)),
                  pl.BlockSpec((d, tn), lambda j: (0, j)),
                  pl.BlockSpec((1, tn), lambda j: (0, j))],
        out_specs=pl.BlockSpec((1, tn), lambda j: (0, j)),
        compiler_params=_params(("arbitrary",), 40),
        name="ada",
    )(c_col, w_ada, b_ada)


def _proj_kernel(x_ref, g_ref, sh_ref, sc_ref, w_ref, cb_ref, cos_ref, sin_ref, qg_ref, kg_ref,
                 o_ref, h_scr, *, tn):
    j = pl.program_id(1)

    @pl.when(j == 0)
    def _():
        h_scr[...] = _norm_modulate(x_ref[...], g_ref[...], sh_ref[...], sc_ref[...]).astype(h_scr.dtype)

    acc = _dot(h_scr[...], w_ref[...])

    def head_norm_rope(gain_ref, out_scale):
        for hh in range(tn // HEAD_DIM):
            t = acc[:, hh * HEAD_DIM:(hh + 1) * HEAD_DIM]
            ms = jnp.mean(t * t, axis=-1, keepdims=True)
            y = t * lax.rsqrt(ms + EPS) * gain_ref[...]
            r = y * cos_ref[...] + pltpu.roll(y, HEAD_DIM // 2, axis=1) * sin_ref[...]
            if out_scale != 1.0:
                r = r * out_scale
            o_ref[:, hh * HEAD_DIM:(hh + 1) * HEAD_DIM] = r.astype(o_ref.dtype)

    @pl.when(j < K_COL // tn)
    def _():
        head_norm_rope(qg_ref, HEAD_DIM ** -0.5 * LOG2E)

    @pl.when(jnp.logical_and(j >= K_COL // tn, j < V_COL // tn))
    def _():
        head_norm_rope(kg_ref, 1.0)

    @pl.when(jnp.logical_and(j >= V_COL // tn, j < G_COL // tn))
    def _():
        o_ref[...] = acc.astype(o_ref.dtype)

    @pl.when(j >= G_COL // tn)
    def _():
        o_ref[...] = _sigmoid(acc + cb_ref[...]).astype(o_ref.dtype)


def _proj(x2, g, sh, sc, w_cat, colbias, cos_t, sin_t, qg, kg):
    s, d = x2.shape
    tm, tn = 1024, 512
    vec = lambda: pl.BlockSpec((1, d), lambda i, j: (0, 0))
    return pl.pallas_call(
        functools.partial(_proj_kernel, tn=tn),
        out_shape=jax.ShapeDtypeStruct((s, PROJ_W), jnp.bfloat16),
        grid=(s // tm, PROJ_W // tn),
        in_specs=[pl.BlockSpec((tm, d), lambda i, j: (i, 0)),
                  vec(), vec(), vec(),
                  pl.BlockSpec((d, tn), lambda i, j: (0, j)),
                  pl.BlockSpec((1, tn), lambda i, j: (0, j)),
                  pl.BlockSpec((tm, HEAD_DIM), lambda i, j: (i, 0)),
                  pl.BlockSpec((tm, HEAD_DIM), lambda i, j: (i, 0)),
                  pl.BlockSpec((1, HEAD_DIM), lambda i, j: (0, 0)),
                  pl.BlockSpec((1, HEAD_DIM), lambda i, j: (0, 0))],
        out_specs=pl.BlockSpec((tm, tn), lambda i, j: (i, j)),
        scratch_shapes=[pltpu.VMEM((tm, d), jnp.bfloat16)],
        compiler_params=_params(("arbitrary", "arbitrary"), 48),
        name="proj",
    )(x2, g, sh, sc, w_cat, colbias, cos_t, sin_t, qg, kg)


def _select_kernel(q_ref, k_ref, bias_ref, km_scr, *, tq):
    i = pl.program_id(1)
    nb = k_ref.shape[0] // MOBA_BLOCK

    @pl.when(i == 0)
    def _():
        k = k_ref[...].astype(jnp.float32).reshape(nb, MOBA_BLOCK, HEAD_DIM)
        km = jnp.sum(k, axis=1) * (1.0 / MOBA_BLOCK)
        km = jnp.concatenate([km, jnp.zeros((LANES - nb, HEAD_DIM), jnp.float32)], axis=0)
        p0 = km.astype(jnp.bfloat16)
        r1 = km - p0.astype(jnp.float32)
        p1 = r1.astype(jnp.bfloat16)
        p2 = (r1 - p1.astype(jnp.float32)).astype(jnp.bfloat16)
        km_scr[0] = p0
        km_scr[1] = p1
        km_scr[2] = p2

    q = q_ref[...]
    gate = _dot_nt(km_scr[0], q) + _dot_nt(km_scr[1], q) + _dot_nt(km_scr[2], q)
    blk = lax.broadcasted_iota(jnp.int32, gate.shape, 0)
    qpos = i * tq + lax.broadcasted_iota(jnp.int32, gate.shape, 1)
    own = lax.shift_right_logical(qpos, BLOCK_SHIFT)
    past = blk < own
    g = jnp.where(past, gate, NEG)
    bias = jnp.where(blk == own, 0.0, NEG)
    for _ in range(MOBA_TOPK):
        top = jnp.max(g, axis=0, keepdims=True)
        first = jnp.min(jnp.where(g == top, blk, LANES), axis=0, keepdims=True)
        pick = blk == first
        bias = jnp.where(jnp.logical_and(pick, past), 0.0, bias)
        g = jnp.where(pick, -jnp.inf, g)
    bias_ref[...] = bias[:nb, :]


def _eye(n):
    return (lax.broadcasted_iota(jnp.int32, (n, n), 0)
            == lax.broadcasted_iota(jnp.int32, (n, n), 1)).astype(jnp.bfloat16)


def _select(proj):
    s = proj.shape[0]
    tq = 1024
    kcol = K_COL // HEAD_DIM
    return pl.pallas_call(
        functools.partial(_select_kernel, tq=tq),
        out_shape=jax.ShapeDtypeStruct((N_HEADS, s // MOBA_BLOCK, s), jnp.float32),
        grid=(N_HEADS, s // tq),
        in_specs=[pl.BlockSpec((tq, HEAD_DIM), lambda h, i: (i, h)),
                  pl.BlockSpec((s, HEAD_DIM), lambda h, i: (0, kcol + h))],
        out_specs=pl.BlockSpec((None, s // MOBA_BLOCK, tq), lambda h, i: (h, 0, i)),
        scratch_shapes=[pltpu.VMEM((3, LANES, HEAD_DIM), jnp.bfloat16)],
        compiler_params=_params(("arbitrary", "arbitrary"), 48),
        name="select",
    )(proj, proj)


ATTN_QB = 4
ATTN_UNROLL = 4
ONES_ROWS = 16
XPOSE_CHUNK = 2048


def _attn_kernel(q_ref, bias_ref, k_ref, v_ref, o_ref, vaug_scr, acc_scr, s_scr, p_scr, ptail_scr):
    i = pl.program_id(1)
    s_len = k_ref.shape[0]
    blk = MOBA_BLOCK

    @pl.when(i == 0)
    def _():
        for c in range(s_len // XPOSE_CHUNK):
            keys = slice(c * XPOSE_CHUNK, (c + 1) * XPOSE_CHUNK)
            vaug_scr[:HEAD_DIM, keys] = _dot_nt(_eye(HEAD_DIM), v_ref[keys, :]).astype(vaug_scr.dtype)
        vaug_scr[HEAD_DIM:, :] = jnp.ones((ONES_ROWS, s_len), vaug_scr.dtype)

    ng = ATTN_QB
    n_past = i * ng
    groups = [slice(g * blk, (g + 1) * blk) for g in range(ng)]
    key_le_query = (lax.broadcasted_iota(jnp.int32, (blk, blk), 0)
                    <= lax.broadcasted_iota(jnp.int32, (blk, blk), 1))

    def score_tile(j, slot, g, causal):
        kb = k_ref[pl.ds(pl.multiple_of(j * blk, blk), blk), :]
        s_t = _dot_nt(kb, q_ref[groups[g], :])
        if causal:
            s_t = jnp.where(key_le_query, s_t, NEG)
        s_scr[slot, :, groups[g]] = s_t
        return jnp.max(s_t, axis=0, keepdims=True)

    def softmax_tile(j, slot, g, m_old, mx):
        b = bias_ref[pl.ds(j, 1), groups[g]]
        m_new = jnp.maximum(m_old, mx + b)
        shift = jnp.maximum(m_new - b, mx)
        p_scr[slot, :, groups[g]] = jnp.exp2(s_scr[slot, :, groups[g]] - shift).astype(p_scr.dtype)
        return m_new, jnp.exp2(m_old - m_new)

    def value_tile(j, slot, g, alpha):
        vb = vaug_scr[:, pl.ds(pl.multiple_of(j * blk, blk), blk)]
        acc_scr[:, groups[g]] = alpha * acc_scr[:, groups[g]] + _dot(vb, p_scr[slot, :, groups[g]])

    def past_step(j, slot, m, mx, a_prev):
        mx_next = tuple(score_tile(j + 1, 1 - slot, g, False) for g in range(ng))
        for g in range(ng):
            value_tile(jnp.maximum(j - 1, 0), 1 - slot, g, a_prev[g])
        m_alpha = [softmax_tile(j, slot, g, m[g], mx[g]) for g in range(ng)]
        return tuple(ma[0] for ma in m_alpha), mx_next, tuple(ma[1] for ma in m_alpha)

    def body(jj, carry):
        for u in range(ATTN_UNROLL):
            carry = past_step(ATTN_UNROLL * jj + u, u % 2, *carry)
        return carry

    acc_scr[...] = jnp.zeros_like(acc_scr)
    p_scr[1] = jnp.zeros(p_scr.shape[1:], p_scr.dtype)
    init = (tuple(jnp.full((1, blk), NEG, jnp.float32) for _ in range(ng)),
            tuple(score_tile(0, 0, g, False) for g in range(ng)),
            tuple(jnp.ones((1, blk), jnp.float32) for _ in range(ng)))
    m, mx, alpha = lax.fori_loop(0, n_past // ATTN_UNROLL, body, init)
    m, mx, alpha = list(m), list(mx), list(alpha)

    for g in range(ng):
        value_tile(jnp.maximum(n_past - 1, 0), 1, g, alpha[g])
    mxs = {(g, 0): mx[g] for g in range(1, ng)}
    mxs[(0, 0)] = score_tile(n_past, 0, 0, True)
    for t in range(1, ng):
        for g in range(t, ng):
            mxs[(g, t)] = score_tile(n_past + t, t, g, g == t)
    span0 = pl.multiple_of(n_past * blk, blk)
    for g in range(ng):
        bs = [bias_ref[pl.ds(n_past + t, 1), groups[g]] for t in range(g + 1)]
        m_new = m[g]
        for t in range(g + 1):
            m_new = jnp.maximum(m_new, mxs[(g, t)] + bs[t])
        for t in range(g + 1):
            shift = jnp.maximum(m_new - bs[t], mxs[(g, t)])
            ptail_scr[t * blk:(t + 1) * blk, groups[g]] = (
                jnp.exp2(s_scr[t, :, groups[g]] - shift).astype(ptail_scr.dtype))
        pv = _dot(vaug_scr[:, pl.ds(span0, (g + 1) * blk)], ptail_scr[:(g + 1) * blk, groups[g]])
        acc_scr[:, groups[g]] = jnp.exp2(m[g] - m_new) * acc_scr[:, groups[g]] + pv
    o_ref[...] = (acc_scr[:HEAD_DIM, :] / acc_scr[HEAD_DIM:HEAD_DIM + 1, :]).T.astype(o_ref.dtype)


def _attention(proj, bias):
    s = proj.shape[0]
    tq = ATTN_QB * MOBA_BLOCK
    kcol, vcol = K_COL // HEAD_DIM, V_COL // HEAD_DIM
    return pl.pallas_call(
        _attn_kernel,
        out_shape=jax.ShapeDtypeStruct((s, ATTN_W), jnp.bfloat16),
        grid=(N_HEADS, s // tq),
        in_specs=[pl.BlockSpec((tq, HEAD_DIM), lambda h, i: (i, h)),
                  pl.BlockSpec((None, s // MOBA_BLOCK, tq), lambda h, i: (h, 0, i)),
                  pl.BlockSpec((s, HEAD_DIM), lambda h, i: (0, kcol + h)),
                  pl.BlockSpec((s, HEAD_DIM), lambda h, i: (0, vcol + h))],
        out_specs=pl.BlockSpec((tq, HEAD_DIM), lambda h, i: (i, h)),
        scratch_shapes=[pltpu.VMEM((HEAD_DIM + ONES_ROWS, s), jnp.bfloat16),
                        pltpu.VMEM((HEAD_DIM + ONES_ROWS, tq), jnp.float32),
                        pltpu.VMEM((ATTN_QB, MOBA_BLOCK, tq), jnp.float32),
                        pltpu.VMEM((2, MOBA_BLOCK, tq), jnp.bfloat16),
                        pltpu.VMEM((tq, tq), jnp.bfloat16)],
        compiler_params=_params(("arbitrary", "arbitrary"), 56),
        name="attention",
    )(proj, bias, proj, proj)


POOL_HALO = 16


def _pool_kernel(u_ref, halo_ref, w_ref, ls_ref, o_ref):
    i = pl.program_id(0)
    tm = u_ref.shape[0]
    u = u_ref[...].astype(jnp.float32)
    halo = jnp.where(i == 0, 0.0, halo_ref[...].astype(jnp.float32))
    t = i * tm + lax.broadcasted_iota(jnp.int32, (tm, 1), 0)
    for g, win in enumerate(POOL_WINDOWS):
        cols = slice(g * POOL_GW, (g + 1) * POOL_GW)
        ug = u[:, cols]
        e = jnp.concatenate([halo[:, cols], ug], axis=0)
        span = 1
        while span < win:
            e = e + pltpu.roll(e, span, axis=0)
            span *= 2
        cnt = jnp.minimum(t + 1, win).astype(jnp.float32)
        dg = e[POOL_HALO:, :] / cnt - ug
        mixed = _dot(dg.astype(jnp.bfloat16), w_ref[g])
        o_ref[:, cols] = (mixed * ls_ref[:, cols]).astype(o_ref.dtype)


def _pool(proj, w_grp, ls):
    s = proj.shape[0]
    tm = 1024
    ucol = U_COL // POOL_W
    return pl.pallas_call(
        _pool_kernel,
        out_shape=jax.ShapeDtypeStruct((s, POOL_W), jnp.bfloat16),
        grid=(s // tm,),
        in_specs=[pl.BlockSpec((tm, POOL_W), lambda i: (i, ucol)),
                  pl.BlockSpec((POOL_HALO, POOL_W),
                               lambda i: (jnp.maximum(i * (tm // POOL_HALO) - 1, 0), ucol)),
                  pl.BlockSpec(w_grp.shape, lambda i: (0, 0, 0)),
                  pl.BlockSpec((1, POOL_W), lambda i: (0, 0))],
        out_specs=pl.BlockSpec((tm, POOL_W), lambda i: (i, 0)),
        compiler_params=_params(("arbitrary",), 48),
        name="pool",
    )(proj, proj, w_grp, ls)


def _merge_kernel(a_ref, p_ref, ga_ref, gp_ref, x_ref, g1_ref, wa_ref, wp_ref, wo_ref, o_ref):
    j = pl.program_id(1)

    @pl.when(j == 0)
    def _():
        o_ref[...] = jnp.zeros_like(o_ref)

    ya = _dot(a_ref[...], wa_ref[...])
    yp = _dot(p_ref[...], wp_ref[...])
    merged = ga_ref[...].astype(jnp.float32) * ya + gp_ref[...].astype(jnp.float32) * yp
    o_ref[...] += _dot(merged.astype(jnp.bfloat16), wo_ref[...])

    @pl.when(j == pl.num_programs(1) - 1)
    def _():
        o_ref[...] = x_ref[...] + g1_ref[...] * o_ref[...]


def _merge(attn, pooled, proj, x2, g1, wa, wp, wo):
    s, d = x2.shape
    tm, tn = 512, 512
    ga0, gp0 = G_COL // tn, (G_COL + d) // tn
    return pl.pallas_call(
        _merge_kernel,
        out_shape=jax.ShapeDtypeStruct((s, d), jnp.float32),
        grid=(s // tm, d // tn),
        in_specs=[pl.BlockSpec((tm, ATTN_W), lambda i, j: (i, 0)),
                  pl.BlockSpec((tm, POOL_W), lambda i, j: (i, 0)),
                  pl.BlockSpec((tm, tn), lambda i, j: (i, ga0 + j)),
                  pl.BlockSpec((tm, tn), lambda i, j: (i, gp0 + j)),
                  pl.BlockSpec((tm, d), lambda i, j: (i, 0)),
                  pl.BlockSpec((1, d), lambda i, j: (0, 0)),
                  pl.BlockSpec((ATTN_W, tn), lambda i, j: (0, j)),
                  pl.BlockSpec((POOL_W, tn), lambda i, j: (0, j)),
                  pl.BlockSpec((tn, d), lambda i, j: (j, 0))],
        out_specs=pl.BlockSpec((tm, d), lambda i, j: (i, 0)),
        compiler_params=_params(("arbitrary", "arbitrary"), 48),
        name="merge",
    )(attn, pooled, proj, proj, x2, g1, wa, wp, wo)


def _ffn_kernel(x_ref, g_ref, sh_ref, sc_ref, g2_ref, wua_ref, wub_ref, cwa_ref, cwb_ref, cba_ref, cbb_ref,
                wd_ref, o_ref, h_scr, tail_scr):
    i = pl.program_id(0)
    j = pl.program_id(1)
    tm = x_ref.shape[0]

    @pl.when(j == 0)
    def _():
        h_scr[...] = _norm_modulate(x_ref[...], g_ref[...], sh_ref[...], sc_ref[...]).astype(h_scr.dtype)
        o_ref[...] = jnp.zeros_like(o_ref)

    @pl.when(i == 0)
    def _():
        tail_scr[j] = jnp.zeros(tail_scr.shape[1:], tail_scr.dtype)

    h = h_scr[...]

    def conv(w_ref, cw_ref, cb_ref, half):
        up = _dot(h, w_ref[...])
        ext = jnp.concatenate([tail_scr[j, half], up], axis=0)
        tail_scr[j, half] = up[tm - SUBLANES:, :]
        out = cb_ref[...] + cw_ref[CONV_W - 1:CONV_W, :] * up
        for back in range(1, CONV_W):
            tap = CONV_W - 1 - back
            out = out + cw_ref[tap:tap + 1, :] * pltpu.roll(ext, back, axis=0)[SUBLANES:, :]
        return out

    a = conv(wua_ref, cwa_ref, cba_ref, 0)
    b = conv(wub_ref, cwb_ref, cbb_ref, 1)
    act = (a * _sigmoid(a) * b).astype(jnp.bfloat16)
    o_ref[...] += _dot(act, wd_ref[...])

    @pl.when(j == pl.num_programs(1) - 1)
    def _():
        o_ref[...] = x_ref[...] + g2_ref[...] * o_ref[...]


def _ffn(x1, g, sh, sc, g2, w_up, conv_w, conv_b, w_down):
    s, d = x1.shape
    tm, tf = 512, 512
    nf = D_FF // tf
    vec = lambda: pl.BlockSpec((1, d), lambda i, j: (0, 0))
    return pl.pallas_call(
        _ffn_kernel,
        out_shape=jax.ShapeDtypeStruct((s, d), jnp.float32),
        grid=(s // tm, nf),
        in_specs=[pl.BlockSpec((tm, d), lambda i, j: (i, 0)),
                  vec(), vec(), vec(), vec(),
                  pl.BlockSpec((d, tf), lambda i, j: (0, j)),
                  pl.BlockSpec((d, tf), lambda i, j: (0, nf + j)),
                  pl.BlockSpec((CONV_W, tf), lambda i, j: (0, j)),
                  pl.BlockSpec((CONV_W, tf), lambda i, j: (0, nf + j)),
                  pl.BlockSpec((1, tf), lambda i, j: (0, j)),
                  pl.BlockSpec((1, tf), lambda i, j: (0, nf + j)),
                  pl.BlockSpec((tf, d), lambda i, j: (j, 0))],
        out_specs=pl.BlockSpec((tm, d), lambda i, j: (i, 0)),
        scratch_shapes=[pltpu.VMEM((tm, d), jnp.bfloat16),
                        pltpu.VMEM((nf, 2, SUBLANES, tf), jnp.float32)],
        compiler_params=_params(("arbitrary", "arbitrary"), 48),
        name="ffn",
    )(x1, g, sh, sc, g2, w_up, w_up, conv_w, conv_w, conv_b, conv_b, w_down)


def _rope_tables(s):
    half = HEAD_DIM // 2
    inv = ROPE_THETA ** (-jnp.arange(half, dtype=jnp.float32) / half)
    ang = jnp.arange(s).astype(jnp.float32)[:, None] * inv[None, :]
    cos, sin = jnp.cos(ang), jnp.sin(ang)
    return jnp.concatenate([cos, cos], axis=1), jnp.concatenate([-sin, sin], axis=1)


def kernel(x, c, w_ada, b_ada, norm_mix_g, w_in, q_norm_g, k_norm_g, w_pool_grp, pool_scale, w_attn_br,
           w_pool_br, w_gate, b_gate, w_o, norm_ffn_g, w_up, conv_w, conv_b, w_down):
    batch, s, d = x.shape
    assert batch == 1 and d == D_MODEL and w_ada.shape[0] == 1
    bf = jnp.bfloat16
    row = lambda v: v.reshape(1, -1)
    cos_t, sin_t = _rope_tables(s)

    mod = _ada(c.reshape(d, 1), w_ada[0], row(b_ada[0]))
    sh1, sc1, g1, sh2, sc2, g2 = [mod[:, n * d:(n + 1) * d] for n in range(6)]

    w_cat = jnp.concatenate([w_in[0], w_gate[0]], axis=1).astype(bf)
    colbias = jnp.concatenate([jnp.zeros((1, G_COL), jnp.float32), row(b_gate[0])], axis=1)
    x2 = x[0]
    proj = _proj(x2, row(norm_mix_g[0]), sh1, sc1, w_cat, colbias, cos_t, sin_t,
                 row(q_norm_g[0]), row(k_norm_g[0]))
    bias = _select(proj)
    attn = _attention(proj, bias)
    pooled = _pool(proj, w_pool_grp[0].astype(bf), row(pool_scale[0]))
    x1 = _merge(attn, pooled, proj, x2, g1, w_attn_br[0].astype(bf), w_pool_br[0].astype(bf),
                w_o[0].astype(bf))
    out = _ffn(x1, row(norm_ffn_g[0]), sh2, sc2, g2, w_up[0].astype(bf), conv_w[0], row(conv_b[0]),
               w_down[0].astype(bf))
    return out[None]
```

```python
import functools
import math

import jax
import jax.numpy as jnp
from jax import lax
from jax.experimental import pallas as pl
from jax.experimental.pallas import tpu as pltpu

D_MODEL = 2048
N_HEADS = 8
HEAD_DIM = 128
ATTN_W = N_HEADS * HEAD_DIM
MOBA_BLOCK = 256
MOBA_TOPK = 3
POOL_WINDOWS = (2, 4, 8, 16)
POOL_W = 1024
POOL_GW = POOL_W // len(POOL_WINDOWS)
D_FF = 5632
CONV_W = 3
ROPE_THETA = 10000.0
EPS = 1e-6
NEG = -1e30
BLOCK_SHIFT = MOBA_BLOCK.bit_length() - 1
LOG2E = math.log2(math.e)

LANES = 128
SUBLANES = 8
PROJ_W = 4 * ATTN_W + 2 * D_MODEL
Q_COL, K_COL, V_COL, U_COL, G_COL = 0, ATTN_W, 2 * ATTN_W, 3 * ATTN_W, 4 * ATTN_W

MIB = 1024 * 1024


def _dot(a, b):
    return jnp.dot(a, b, preferred_element_type=jnp.float32)


def _dot_nt(a, b):
    return lax.dot_general(a, b, (((1,), (1,)), ((), ())), preferred_element_type=jnp.float32)


def _sigmoid(x):
    return 0.5 * jnp.tanh(0.5 * x) + 0.5


def _params(semantics, vmem_mib):
    return pltpu.CompilerParams(dimension_semantics=semantics, vmem_limit_bytes=vmem_mib * MIB)


def _norm_modulate(x, g, shift, scale):
    ms = jnp.mean(x * x, axis=-1, keepdims=True)
    y = x * lax.rsqrt(ms + EPS) * g
    return y * (1.0 + scale) + shift


def _ada_kernel(c_ref, w_ref, b_ref, o_ref):
    c = c_ref[...]
    s = c * jax.nn.sigmoid(c)
    o_ref[...] = jnp.sum(s * w_ref[...], axis=0, keepdims=True) + b_ref[...]


def _ada(c_col, w_ada, b_ada):
    d, n = w_ada.shape
    tn = 1024
    return pl.pallas_call(
        _ada_kernel,
        out_shape=jax.ShapeDtypeStruct((1, n), jnp.float32),
        grid=(n // tn,),
        in_specs=[pl.BlockSpec((d, 1), lambda j: (0, 0)),
                  pl.BlockSpec((d, tn), lambda j: (0, j)),
                  pl.BlockSpec((1, tn), lambda j: (0, j))],
        out_specs=pl.BlockSpec((1, tn), lambda j: (0, j)),
        compiler_params=_params(("arbitrary",), 40),
        name="ada",
    )(c_col, w_ada, b_ada)


def _proj_kernel(x_ref, g_ref, sh_ref, sc_ref, wi_ref, wg_ref, cb_ref, cos_ref, sin_ref, qg_ref, kg_ref,
                 o_ref, h_scr, acc_scr, *, tn):
    j = pl.program_id(1)

    @pl.when(j == 0)
    def _():
        h_scr[...] = _norm_modulate(x_ref[...], g_ref[...], sh_ref[...], sc_ref[...]).astype(h_scr.dtype)

    @pl.when(j < G_COL // tn)
    def _():
        acc_scr[...] = _dot(h_scr[...], wi_ref[...])

    @pl.when(j >= G_COL // tn)
    def _():
        acc_scr[...] = _dot(h_scr[...], wg_ref[...])

    def head_norm_rope(gain_ref, out_scale):
        for hh in range(tn // HEAD_DIM):
            t = acc_scr[:, hh * HEAD_DIM:(hh + 1) * HEAD_DIM]
            ms = jnp.mean(t * t, axis=-1, keepdims=True)
            y = t * lax.rsqrt(ms + EPS) * gain_ref[...]
            r = y * cos_ref[...] + pltpu.roll(y, HEAD_DIM // 2, axis=1) * sin_ref[...]
            if out_scale != 1.0:
                r = r * out_scale
            o_ref[:, hh * HEAD_DIM:(hh + 1) * HEAD_DIM] = r.astype(o_ref.dtype)

    @pl.when(j < K_COL // tn)
    def _():
        head_norm_rope(qg_ref, HEAD_DIM ** -0.5 * LOG2E)

    @pl.when(jnp.logical_and(j >= K_COL // tn, j < V_COL // tn))
    def _():
        head_norm_rope(kg_ref, 1.0)

    @pl.when(jnp.logical_and(j >= V_COL // tn, j < G_COL // tn))
    def _():
        o_ref[...] = acc_scr[...].astype(o_ref.dtype)

    @pl.when(j >= G_COL // tn)
    def _():
        o_ref[...] = _sigmoid(acc_scr[...] + cb_ref[...]).astype(o_ref.dtype)


def _proj(x2, g, sh, sc, w_in, w_gate, b_gate, cos_t, sin_t, qg, kg):
    s, d = x2.shape
    tm, tn = 1024, 512
    n_in = G_COL // tn
    vec = lambda: pl.BlockSpec((1, d), lambda i, j: (0, 0))
    return pl.pallas_call(
        functools.partial(_proj_kernel, tn=tn),
        out_shape=jax.ShapeDtypeStruct((s, PROJ_W), jnp.bfloat16),
        grid=(s // tm, PROJ_W // tn),
        in_specs=[pl.BlockSpec((tm, d), lambda i, j: (i, 0)),
                  vec(), vec(), vec(),
                  pl.BlockSpec((d, tn), lambda i, j: (0, jnp.minimum(j, n_in - 1))),
                  pl.BlockSpec((d, tn), lambda i, j: (0, jnp.maximum(j - n_in, 0))),
                  pl.BlockSpec((1, tn), lambda i, j: (0, jnp.maximum(j - n_in, 0))),
                  pl.BlockSpec((tm, HEAD_DIM), lambda i, j: (i, 0)),
                  pl.BlockSpec((tm, HEAD_DIM), lambda i, j: (i, 0)),
                  pl.BlockSpec((1, HEAD_DIM), lambda i, j: (0, 0)),
                  pl.BlockSpec((1, HEAD_DIM), lambda i, j: (0, 0))],
        out_specs=pl.BlockSpec((tm, tn), lambda i, j: (i, j)),
        scratch_shapes=[pltpu.VMEM((tm, d), jnp.bfloat16),
                        pltpu.VMEM((tm, tn), jnp.float32)],
        compiler_params=_params(("arbitrary", "arbitrary"), 48),
        name="proj",
    )(x2, g, sh, sc, w_in, w_gate, b_gate, cos_t, sin_t, qg, kg)


def _select_kernel(q_ref, k_ref, bias_ref, km_scr, *, tq):
    i = pl.program_id(1)
    nb = k_ref.shape[0] // MOBA_BLOCK

    @pl.when(i == 0)
    def _():
        k = k_ref[...].astype(jnp.float32).reshape(nb, MOBA_BLOCK, HEAD_DIM)
        km = jnp.sum(k, axis=1) * (1.0 / MOBA_BLOCK)
        p0 = km.astype(jnp.bfloat16)
        r1 = km - p0.astype(jnp.float32)
        p1 = r1.astype(jnp.bfloat16)
        p2 = (r1 - p1.astype(jnp.float32)).astype(jnp.bfloat16)
        km_scr[0] = p0
        km_scr[1] = p1
        km_scr[2] = p2

    q = q_ref[...]
    gate = _dot_nt(km_scr[0], q) + _dot_nt(km_scr[1], q) + _dot_nt(km_scr[2], q)
    blk = lax.broadcasted_iota(jnp.int32, gate.shape, 0)
    qpos = i * tq + lax.broadcasted_iota(jnp.int32, gate.shape, 1)
    own = lax.shift_right_logical(qpos, BLOCK_SHIFT)
    past = blk < own
    g = jnp.where(past, gate, NEG)
    bias = jnp.where(blk == own, 0.0, NEG)
    for _ in range(MOBA_TOPK):
        top = jnp.max(g, axis=0, keepdims=True)
        first = jnp.min(jnp.where(g == top, blk, nb), axis=0, keepdims=True)
        pick = blk == first
        bias = jnp.where(jnp.logical_and(pick, past), 0.0, bias)
        g = jnp.where(pick, -jnp.inf, g)
    bias_ref[...] = bias


def _eye(n):
    return (lax.broadcasted_iota(jnp.int32, (n, n), 0)
            == lax.broadcasted_iota(jnp.int32, (n, n), 1)).astype(jnp.bfloat16)


def _select(proj):
    s = proj.shape[0]
    tq = 1024
    kcol = K_COL // HEAD_DIM
    return pl.pallas_call(
        functools.partial(_select_kernel, tq=tq),
        out_shape=jax.ShapeDtypeStruct((N_HEADS, s // MOBA_BLOCK, s), jnp.float32),
        grid=(N_HEADS, s // tq),
        in_specs=[pl.BlockSpec((tq, HEAD_DIM), lambda h, i: (i, h)),
                  pl.BlockSpec((s, HEAD_DIM), lambda h, i: (0, kcol + h))],
        out_specs=pl.BlockSpec((None, s // MOBA_BLOCK, tq), lambda h, i: (h, 0, i)),
        scratch_shapes=[pltpu.VMEM((3, s // MOBA_BLOCK, HEAD_DIM), jnp.bfloat16)],
        compiler_params=_params(("arbitrary", "arbitrary"), 48),
        name="select",
    )(proj, proj)


ATTN_QB = 8
ATTN_UNROLL = 4
ONES_ROWS = 16
XPOSE_CHUNK = 2048


def _attn_kernel(q_ref, bias_ref, k_ref, v_ref, o_ref, vaug_scr, acc_scr, s_scr, p_scr, ptail_scr):
    i = pl.program_id(1)
    s_len = k_ref.shape[0]
    blk = MOBA_BLOCK

    @pl.when(i == 0)
    def _():
        for c in range(s_len // XPOSE_CHUNK):
            keys = slice(c * XPOSE_CHUNK, (c + 1) * XPOSE_CHUNK)
            vaug_scr[:HEAD_DIM, keys] = _dot_nt(_eye(HEAD_DIM), v_ref[keys, :]).astype(vaug_scr.dtype)
        vaug_scr[HEAD_DIM:, :] = jnp.ones((ONES_ROWS, s_len), vaug_scr.dtype)

    ng = ATTN_QB
    n_past = i * ng
    groups = [slice(g * blk, (g + 1) * blk) for g in range(ng)]
    key_le_query = (lax.broadcasted_iota(jnp.int32, (blk, blk), 0)
                    <= lax.broadcasted_iota(jnp.int32, (blk, blk), 1))

    def score_tile(j, slot, g, causal):
        kb = k_ref[pl.ds(pl.multiple_of(j * blk, blk), blk), :]
        s_t = _dot_nt(kb, q_ref[groups[g], :])
        if causal:
            s_t = jnp.where(key_le_query, s_t, NEG)
        s_scr[slot, :, groups[g]] = s_t
        return jnp.max(s_t, axis=0, keepdims=True)

    def softmax_tile(j, slot, g, m_old, mx):
        b = bias_ref[pl.ds(j, 1), groups[g]]
        m_new = jnp.maximum(m_old, mx + b)
        shift = jnp.maximum(m_new - b, mx)
        p_scr[slot, :, groups[g]] = jnp.exp2(s_scr[slot, :, groups[g]] - shift).astype(p_scr.dtype)
        return m_new, jnp.exp2(m_old - m_new)

    def value_tile(j, slot, g, alpha):
        vb = vaug_scr[:, pl.ds(pl.multiple_of(j * blk, blk), blk)]
        acc_scr[:, groups[g]] = alpha * acc_scr[:, groups[g]] + _dot(vb, p_scr[slot, :, groups[g]])

    def past_step(j, slot, m, mx, a_prev):
        mx_next, m_alpha = [], []
        for g in range(ng):
            value_tile(jnp.maximum(j - 1, 0), 1 - slot, g, a_prev[g])
            mx_next.append(score_tile(j + 1, 1 - slot, g, False))
            m_alpha.append(softmax_tile(j, slot, g, m[g], mx[g]))
        mx_next = tuple(mx_next)
        return tuple(ma[0] for ma in m_alpha), mx_next, tuple(ma[1] for ma in m_alpha)

    def body(jj, carry):
        for u in range(ATTN_UNROLL):
            carry = past_step(ATTN_UNROLL * jj + u, u % 2, *carry)
        return carry

    acc_scr[...] = jnp.zeros_like(acc_scr)
    p_scr[1] = jnp.zeros(p_scr.shape[1:], p_scr.dtype)
    init = (tuple(jnp.full((1, blk), NEG, jnp.float32) for _ in range(ng)),
            tuple(score_tile(0, 0, g, False) for g in range(ng)),
            tuple(jnp.ones((1, blk), jnp.float32) for _ in range(ng)))
    m, mx, alpha = lax.fori_loop(0, n_past // ATTN_UNROLL, body, init)
    m, mx, alpha = list(m), list(mx), list(alpha)

    for g in range(ng):
        value_tile(jnp.maximum(n_past - 1, 0), 1, g, alpha[g])
    mxs = {(g, 0): mx[g] for g in range(1, ng)}
    mxs[(0, 0)] = score_tile(n_past, 0, 0, True)
    for t in range(1, ng):
        for g in range(t, ng):
            mxs[(g, t)] = score_tile(n_past + t, t, g, g == t)
    span0 = pl.multiple_of(n_past * blk, blk)
    for g in range(ng):
        bs = [bias_ref[pl.ds(n_past + t, 1), groups[g]] for t in range(g + 1)]
        m_new = m[g]
        for t in range(g + 1):
            m_new = jnp.maximum(m_new, mxs[(g, t)] + bs[t])
        for t in range(g + 1):
            shift = jnp.maximum(m_new - bs[t], mxs[(g, t)])
            ptail_scr[t * blk:(t + 1) * blk, groups[g]] = (
                jnp.exp2(s_scr[t, :, groups[g]] - shift).astype(ptail_scr.dtype))
        pv = _dot(vaug_scr[:, pl.ds(span0, (g + 1) * blk)], ptail_scr[:(g + 1) * blk, groups[g]])
        acc_scr[:, groups[g]] = jnp.exp2(m[g] - m_new) * acc_scr[:, groups[g]] + pv
    o_ref[...] = (acc_scr[:HEAD_DIM, :] / acc_scr[HEAD_DIM:HEAD_DIM + 1, :]).T.astype(o_ref.dtype)


def _attention(proj, bias):
    s = proj.shape[0]
    tq = ATTN_QB * MOBA_BLOCK
    kcol, vcol = K_COL // HEAD_DIM, V_COL // HEAD_DIM
    return pl.pallas_call(
        _attn_kernel,
        out_shape=jax.ShapeDtypeStruct((s, ATTN_W), jnp.bfloat16),
        grid=(N_HEADS, s // tq),
        in_specs=[pl.BlockSpec((tq, HEAD_DIM), lambda h, i: (i, h)),
                  pl.BlockSpec((None, s // MOBA_BLOCK, tq), lambda h, i: (h, 0, i)),
                  pl.BlockSpec((s, HEAD_DIM), lambda h, i: (0, kcol + h)),
                  pl.BlockSpec((s, HEAD_DIM), lambda h, i: (0, vcol + h))],
        out_specs=pl.BlockSpec((tq, HEAD_DIM), lambda h, i: (i, h)),
        scratch_shapes=[pltpu.VMEM((HEAD_DIM + ONES_ROWS, s), jnp.bfloat16),
                        pltpu.VMEM((HEAD_DIM + ONES_ROWS, tq), jnp.float32),
                        pltpu.VMEM((ATTN_QB, MOBA_BLOCK, tq), jnp.float32),
                        pltpu.VMEM((2, MOBA_BLOCK, tq), jnp.bfloat16),
                        pltpu.VMEM((tq, tq), jnp.bfloat16)],
        compiler_params=_params(("arbitrary", "arbitrary"), 56),
        name="attention",
    )(proj, bias, proj, proj)


POOL_HALO = 16


def _pool_kernel(u_ref, halo_ref, w_ref, ls_ref, o_ref):
    i = pl.program_id(0)
    tm = u_ref.shape[0]
    u = u_ref[...].astype(jnp.float32)
    halo = jnp.where(i == 0, 0.0, halo_ref[...].astype(jnp.float32))
    t = i * tm + lax.broadcasted_iota(jnp.int32, (tm, 1), 0)
    for g, win in enumerate(POOL_WINDOWS):
        cols = slice(g * POOL_GW, (g + 1) * POOL_GW)
        ug = u[:, cols]
        e = jnp.concatenate([halo[:, cols], ug], axis=0)
        span = 1
        while span < win:
            e = e + pltpu.roll(e, span, axis=0)
            span *= 2
        cnt = jnp.minimum(t + 1, win).astype(jnp.float32)
        dg = e[POOL_HALO:, :] / cnt - ug
        mixed = _dot(dg.astype(jnp.bfloat16), w_ref[g])
        o_ref[:, cols] = (mixed * ls_ref[:, cols]).astype(o_ref.dtype)


def _pool(proj, w_grp, ls):
    s = proj.shape[0]
    tm = 1024
    ucol = U_COL // POOL_W
    return pl.pallas_call(
        _pool_kernel,
        out_shape=jax.ShapeDtypeStruct((s, POOL_W), jnp.bfloat16),
        grid=(s // tm,),
        in_specs=[pl.BlockSpec((tm, POOL_W), lambda i: (i, ucol)),
                  pl.BlockSpec((POOL_HALO, POOL_W),
                               lambda i: (jnp.maximum(i * (tm // POOL_HALO) - 1, 0), ucol)),
                  pl.BlockSpec(w_grp.shape, lambda i: (0, 0, 0)),
                  pl.BlockSpec((1, POOL_W), lambda i: (0, 0))],
        out_specs=pl.BlockSpec((tm, POOL_W), lambda i: (i, 0)),
        compiler_params=_params(("arbitrary",), 48),
        name="pool",
    )(proj, proj, w_grp, ls)


def _merge_kernel(a_ref, p_ref, ga_ref, gp_ref, x_ref, g1_ref, wa_ref, wp_ref, wo_ref, o_ref):
    j = pl.program_id(1)

    @pl.when(j == 0)
    def _():
        o_ref[...] = jnp.zeros_like(o_ref)

    ya = _dot(a_ref[...], wa_ref[...])
    yp = _dot(p_ref[...], wp_ref[...])
    merged = ga_ref[...].astype(jnp.float32) * ya + gp_ref[...].astype(jnp.float32) * yp
    o_ref[...] += _dot(merged.astype(jnp.bfloat16), wo_ref[...])

    @pl.when(j == pl.num_programs(1) - 1)
    def _():
        o_ref[...] = x_ref[...] + g1_ref[...] * o_ref[...]


def _merge(attn, pooled, proj, x2, g1, wa, wp, wo):
    s, d = x2.shape
    tm, tn = 512, 512
    ga0, gp0 = G_COL // tn, (G_COL + d) // tn
    return pl.pallas_call(
        _merge_kernel,
        out_shape=jax.ShapeDtypeStruct((s, d), jnp.float32),
        grid=(s // tm, d // tn),
        in_specs=[pl.BlockSpec((tm, ATTN_W), lambda i, j: (i, 0)),
                  pl.BlockSpec((tm, POOL_W), lambda i, j: (i, 0)),
                  pl.BlockSpec((tm, tn), lambda i, j: (i, ga0 + j)),
                  pl.BlockSpec((tm, tn), lambda i, j: (i, gp0 + j)),
                  pl.BlockSpec((tm, d), lambda i, j: (i, 0)),
                  pl.BlockSpec((1, d), lambda i, j: (0, 0)),
                  pl.BlockSpec((ATTN_W, tn), lambda i, j: (0, j)),
                  pl.BlockSpec((POOL_W, tn), lambda i, j: (0, j)),
                  pl.BlockSpec((tn, d), lambda i, j: (j, 0))],
        out_specs=pl.BlockSpec((tm, d), lambda i, j: (i, 0)),
        compiler_params=_params(("arbitrary", "arbitrary"), 48),
        name="merge",
    )(attn, pooled, proj, proj, x2, g1, wa, wp, wo)


def _ffn_kernel(x_ref, g_ref, sh_ref, sc_ref, g2_ref, wua_ref, wub_ref, cwa_ref, cwb_ref, cba_ref, cbb_ref,
                wd_ref, o_ref, h_scr, tail_scr):
    i = pl.program_id(0)
    j = pl.program_id(1)
    tm = x_ref.shape[0]

    @pl.when(j == 0)
    def _():
        h_scr[...] = _norm_modulate(x_ref[...], g_ref[...], sh_ref[...], sc_ref[...]).astype(h_scr.dtype)
        o_ref[...] = jnp.zeros_like(o_ref)

    @pl.when(i == 0)
    def _():
        tail_scr[j] = jnp.zeros(tail_scr.shape[1:], tail_scr.dtype)

    h = h_scr[...]

    def conv(w_ref, cw_ref, cb_ref, half):
        up = _dot(h, w_ref[...])
        ext = jnp.concatenate([tail_scr[j, half], up], axis=0)
        tail_scr[j, half] = up[tm - SUBLANES:, :]
        out = cb_ref[...] + cw_ref[CONV_W - 1:CONV_W, :] * up
        for back in range(1, CONV_W):
            tap = CONV_W - 1 - back
            out = out + cw_ref[tap:tap + 1, :] * pltpu.roll(ext, back, axis=0)[SUBLANES:, :]
        return out

    a = conv(wua_ref, cwa_ref, cba_ref, 0)
    b = conv(wub_ref, cwb_ref, cbb_ref, 1)
    act = (a * _sigmoid(a) * b).astype(jnp.bfloat16)
    o_ref[...] += _dot(act, wd_ref[...])

    @pl.when(j == pl.num_programs(1) - 1)
    def _():
        o_ref[...] = x_ref[...] + g2_ref[...] * o_ref[...]


def _ffn(x1, g, sh, sc, g2, w_up, conv_w, conv_b, w_down):
    s, d = x1.shape
    tm, tf = 512, 512
    nf = D_FF // tf
    vec = lambda: pl.BlockSpec((1, d), lambda i, j: (0, 0))
    return pl.pallas_call(
        _ffn_kernel,
        out_shape=jax.ShapeDtypeStruct((s, d), jnp.float32),
        grid=(s // tm, nf),
        in_specs=[pl.BlockSpec((tm, d), lambda i, j: (i, 0)),
                  vec(), vec(), vec(), vec(),
                  pl.BlockSpec((d, tf), lambda i, j: (0, j)),
                  pl.BlockSpec((d, tf), lambda i, j: (0, nf + j)),
                  pl.BlockSpec((CONV_W, tf), lambda i, j: (0, j)),
                  pl.BlockSpec((CONV_W, tf), lambda i, j: (0, nf + j)),
                  pl.BlockSpec((1, tf), lambda i, j: (0, j)),
                  pl.BlockSpec((1, tf), lambda i, j: (0, nf + j)),
                  pl.BlockSpec((tf, d), lambda i, j: (j, 0))],
        out_specs=pl.BlockSpec((tm, d), lambda i, j: (i, 0)),
        scratch_shapes=[pltpu.VMEM((tm, d), jnp.bfloat16),
                        pltpu.VMEM((nf, 2, SUBLANES, tf), jnp.float32)],
        compiler_params=_params(("arbitrary", "arbitrary"), 48),
        name="ffn",
    )(x1, g, sh, sc, g2, w_up, w_up, conv_w, conv_w, conv_b, conv_b, w_down)


def _rope_tables(s):
    half = HEAD_DIM // 2
    inv = ROPE_THETA ** (-jnp.arange(half, dtype=jnp.float32) / half)
    ang = jnp.arange(s).astype(jnp.float32)[:, None] * inv[None, :]
    cos, sin = jnp.cos(ang), jnp.sin(ang)
    return jnp.concatenate([cos, cos], axis=1), jnp.concatenate([-sin, sin], axis=1)


def kernel(x, c, w_ada, b_ada, norm_mix_g, w_in, q_norm_g, k_norm_g, w_pool_grp, pool_scale, w_attn_br,
           w_pool_br, w_gate, b_gate, w_o, norm_ffn_g, w_up, conv_w, conv_b, w_down):
    batch, s, d = x.shape
    assert batch == 1 and d == D_MODEL and w_ada.shape[0] == 1
    bf = jnp.bfloat16
    row = lambda v: v.reshape(1, -1)
    cos_t, sin_t = _rope_tables(s)

    mod = _ada(c.reshape(d, 1), w_ada[0], row(b_ada[0]))
    sh1, sc1, g1, sh2, sc2, g2 = [mod[:, n * d:(n + 1) * d] for n in range(6)]

    x2 = x[0]
    proj = _proj(x2, row(norm_mix_g[0]), sh1, sc1, w_in[0].astype(bf), w_gate[0].astype(bf), row(b_gate[0]),
                 cos_t, sin_t, row(q_norm_g[0]), row(k_norm_g[0]))
    bias = _select(proj)
    attn = _attention(proj, bias)
    pooled = _pool(proj, w_pool_grp[0].astype(bf), row(pool_scale[0]))
    x1 = _merge(attn, pooled, proj, x2, g1, w_attn_br[0].astype(bf), w_pool_br[0].astype(bf),
                w_o[0].astype(bf))
    out = _ffn(x1, row(norm_ffn_g[0]), sh2, sc2, g2, w_up[0].astype(bf), conv_w[0], row(conv_b[0]),
               w_down[0].astype(bf))
    return out[None]
```

```python
import functools
import math

import jax
import jax.numpy as jnp
from jax import lax
from jax.experimental import pallas as pl
from jax.experimental.pallas import tpu as pltpu

D_MODEL = 2048
N_HEADS = 8
HEAD_DIM = 128
ATTN_W = N_HEADS * HEAD_DIM
MOBA_BLOCK = 256
MOBA_TOPK = 3
POOL_WINDOWS = (2, 4, 8, 16)
POOL_W = 1024
POOL_GW = POOL_W // len(POOL_WINDOWS)
D_FF = 5632
CONV_W = 3
ROPE_THETA = 10000.0
EPS = 1e-6
NEG = -1e30
BLOCK_SHIFT = MOBA_BLOCK.bit_length() - 1
LOG2E = math.log2(math.e)

LANES = 128
SUBLANES = 8
PROJ_W = 4 * ATTN_W + 2 * D_MODEL
Q_COL, K_COL, V_COL, U_COL, G_COL = 0, ATTN_W, 2 * ATTN_W, 3 * ATTN_W, 4 * ATTN_W

MIB = 1024 * 1024


def _dot(a, b):
    return jnp.dot(a, b, preferred_element_type=jnp.float32)


def _dot_nt(a, b):
    return lax.dot_general(a, b, (((1,), (1,)), ((), ())), preferred_element_type=jnp.float32)


def _sigmoid(x):
    return 0.5 * jnp.tanh(0.5 * x) + 0.5


def _params(semantics, vmem_mib):
    return pltpu.CompilerParams(dimension_semantics=semantics, vmem_limit_bytes=vmem_mib * MIB)


def _norm_modulate(x, g, shift, scale):
    ms = jnp.mean(x * x, axis=-1, keepdims=True)
    y = x * lax.rsqrt(ms + EPS) * g
    return y * (1.0 + scale) + shift


def _ada_kernel(c_ref, w_ref, b_ref, o_ref):
    c = c_ref[...]
    s = c * jax.nn.sigmoid(c)
    o_ref[...] = jnp.sum(s * w_ref[...], axis=0, keepdims=True) + b_ref[...]


def _ada(c_col, w_ada, b_ada):
    d, n = w_ada.shape
    tn = 1024
    return pl.pallas_call(
        _ada_kernel,
        out_shape=jax.ShapeDtypeStruct((1, n), jnp.float32),
        grid=(n // tn,),
        in_specs=[pl.BlockSpec((d, 1), lambda j: (0, 0)),
                  pl.BlockSpec((d, tn), lambda j: (0, j)),
                  pl.BlockSpec((1, tn), lambda j: (0, j))],
        out_specs=pl.BlockSpec((1, tn), lambda j: (0, j)),
        compiler_params=_params(("arbitrary",), 40),
        name="ada",
    )(c_col, w_ada, b_ada)


def _proj_kernel(x_ref, g_ref, sh_ref, sc_ref, wi_ref, wg_ref, cb_ref, cos_ref, sin_ref, qg_ref, kg_ref,
                 o_ref, h_scr, acc_scr, *, tn):
    j = pl.program_id(1)

    @pl.when(j == 0)
    def _():
        h_scr[...] = _norm_modulate(x_ref[...], g_ref[...], sh_ref[...], sc_ref[...]).astype(h_scr.dtype)

    @pl.when(j < G_COL // tn)
    def _():
        acc_scr[...] = _dot(h_scr[...], wi_ref[...])

    @pl.when(j >= G_COL // tn)
    def _():
        acc_scr[...] = _dot(h_scr[...], wg_ref[...])

    def head_norm_rope(gain_ref, out_scale):
        for hh in range(tn // HEAD_DIM):
            t = acc_scr[:, hh * HEAD_DIM:(hh + 1) * HEAD_DIM]
            ms = jnp.mean(t * t, axis=-1, keepdims=True)
            y = t * lax.rsqrt(ms + EPS) * gain_ref[...]
            r = y * cos_ref[...] + pltpu.roll(y, HEAD_DIM // 2, axis=1) * sin_ref[...]
            if out_scale != 1.0:
                r = r * out_scale
            o_ref[:, hh * HEAD_DIM:(hh + 1) * HEAD_DIM] = r.astype(o_ref.dtype)

    @pl.when(j < K_COL // tn)
    def _():
        head_norm_rope(qg_ref, HEAD_DIM ** -0.5 * LOG2E)

    @pl.when(jnp.logical_and(j >= K_COL // tn, j < V_COL // tn))
    def _():
        head_norm_rope(kg_ref, 1.0)

    @pl.when(jnp.logical_and(j >= V_COL // tn, j < G_COL // tn))
    def _():
        o_ref[...] = acc_scr[...].astype(o_ref.dtype)

    @pl.when(j >= G_COL // tn)
    def _():
        o_ref[...] = _sigmoid(acc_scr[...] + cb_ref[...]).astype(o_ref.dtype)


def _proj(x2, g, sh, sc, w_in, w_gate, b_gate, cos_t, sin_t, qg, kg):
    s, d = x2.shape
    tm, tn = 1024, 512
    n_in = G_COL // tn
    vec = lambda: pl.BlockSpec((1, d), lambda i, j: (0, 0))
    return pl.pallas_call(
        functools.partial(_proj_kernel, tn=tn),
        out_shape=jax.ShapeDtypeStruct((s, PROJ_W), jnp.bfloat16),
        grid=(s // tm, PROJ_W // tn),
        in_specs=[pl.BlockSpec((tm, d), lambda i, j: (i, 0)),
                  vec(), vec(), vec(),
                  pl.BlockSpec((d, tn), lambda i, j: (0, jnp.minimum(j, n_in - 1))),
                  pl.BlockSpec((d, tn), lambda i, j: (0, jnp.maximum(j - n_in, 0))),
                  pl.BlockSpec((1, tn), lambda i, j: (0, jnp.maximum(j - n_in, 0))),
                  pl.BlockSpec((tm, HEAD_DIM), lambda i, j: (i, 0)),
                  pl.BlockSpec((tm, HEAD_DIM), lambda i, j: (i, 0)),
                  pl.BlockSpec((1, HEAD_DIM), lambda i, j: (0, 0)),
                  pl.BlockSpec((1, HEAD_DIM), lambda i, j: (0, 0))],
        out_specs=pl.BlockSpec((tm, tn), lambda i, j: (i, j)),
        scratch_shapes=[pltpu.VMEM((tm, d), jnp.bfloat16),
                        pltpu.VMEM((tm, tn), jnp.float32)],
        compiler_params=_params(("arbitrary", "arbitrary"), 48),
        name="proj",
    )(x2, g, sh, sc, w_in, w_gate, b_gate, cos_t, sin_t, qg, kg)


def _select_kernel(q_ref, k_ref, bias_ref, km_scr, *, tq):
    i = pl.program_id(1)
    nb = k_ref.shape[0] // MOBA_BLOCK

    @pl.when(i == 0)
    def _():
        k = k_ref[...].astype(jnp.float32).reshape(nb, MOBA_BLOCK, HEAD_DIM)
        km = jnp.sum(k, axis=1) * (1.0 / MOBA_BLOCK)
        p0 = km.astype(jnp.bfloat16)
        r1 = km - p0.astype(jnp.float32)
        p1 = r1.astype(jnp.bfloat16)
        p2 = (r1 - p1.astype(jnp.float32)).astype(jnp.bfloat16)
        km_scr[0] = p0
        km_scr[1] = p1
        km_scr[2] = p2

    q = q_ref[...]
    gate = _dot_nt(km_scr[0], q) + _dot_nt(km_scr[1], q) + _dot_nt(km_scr[2], q)
    blk = lax.broadcasted_iota(jnp.int32, gate.shape, 0)
    qpos = i * tq + lax.broadcasted_iota(jnp.int32, gate.shape, 1)
    own = lax.shift_right_logical(qpos, BLOCK_SHIFT)
    past = blk < own
    g = jnp.where(past, gate, NEG)
    bias = jnp.where(blk == own, 0.0, NEG)
    for _ in range(MOBA_TOPK):
        top = jnp.max(g, axis=0, keepdims=True)
        first = jnp.min(jnp.where(g == top, blk, nb), axis=0, keepdims=True)
        pick = blk == first
        bias = jnp.where(jnp.logical_and(pick, past), 0.0, bias)
        g = jnp.where(pick, -jnp.inf, g)
    bias_ref[...] = bias


def _eye(n):
    return (lax.broadcasted_iota(jnp.int32, (n, n), 0)
            == lax.broadcasted_iota(jnp.int32, (n, n), 1)).astype(jnp.bfloat16)


def _select(proj):
    s = proj.shape[0]
    tq = 1024
    kcol = K_COL // HEAD_DIM
    return pl.pallas_call(
        functools.partial(_select_kernel, tq=tq),
        out_shape=jax.ShapeDtypeStruct((N_HEADS, s // MOBA_BLOCK, s), jnp.float32),
        grid=(N_HEADS, s // tq),
        in_specs=[pl.BlockSpec((tq, HEAD_DIM), lambda h, i: (i, h)),
                  pl.BlockSpec((s, HEAD_DIM), lambda h, i: (0, kcol + h))],
        out_specs=pl.BlockSpec((None, s // MOBA_BLOCK, tq), lambda h, i: (h, 0, i)),
        scratch_shapes=[pltpu.VMEM((3, s // MOBA_BLOCK, HEAD_DIM), jnp.bfloat16)],
        compiler_params=_params(("arbitrary", "arbitrary"), 48),
        name="select",
    )(proj, proj)


ATTN_QB = 8
ATTN_UNROLL = 4
ONES_ROWS = 16
XPOSE_CHUNK = 2048


def _attn_kernel(q_ref, bias_ref, k_ref, v_ref, o_ref, vaug_scr, acc_scr, s_scr, p_scr, ptail_scr):
    i = pl.program_id(1)
    s_len = k_ref.shape[0]
    blk = MOBA_BLOCK

    @pl.when(i == 0)
    def _():
        for c in range(s_len // XPOSE_CHUNK):
            keys = slice(c * XPOSE_CHUNK, (c + 1) * XPOSE_CHUNK)
            vaug_scr[:HEAD_DIM, keys] = _dot_nt(_eye(HEAD_DIM), v_ref[keys, :]).astype(vaug_scr.dtype)
        vaug_scr[HEAD_DIM:, :] = jnp.ones((ONES_ROWS, s_len), vaug_scr.dtype)

    ng = ATTN_QB
    n_past = i * ng
    groups = [slice(g * blk, (g + 1) * blk) for g in range(ng)]
    key_le_query = (lax.broadcasted_iota(jnp.int32, (blk, blk), 0)
                    <= lax.broadcasted_iota(jnp.int32, (blk, blk), 1))

    def score_tile(j, slot, g, causal):
        kb = k_ref[pl.ds(pl.multiple_of(j * blk, blk), blk), :]
        s_t = _dot_nt(kb, q_ref[groups[g], :])
        if causal:
            s_t = jnp.where(key_le_query, s_t, NEG)
        s_scr[slot, :, groups[g]] = s_t
        return jnp.max(s_t, axis=0, keepdims=True)

    def softmax_tile(j, slot, g, m_old, mx):
        b = bias_ref[pl.ds(j, 1), groups[g]]
        m_new = jnp.maximum(m_old, mx + b)
        shift = jnp.maximum(m_new - b, mx)
        p_scr[slot, :, groups[g]] = jnp.exp2(s_scr[slot, :, groups[g]] - shift).astype(p_scr.dtype)
        return m_new, jnp.exp2(m_old - m_new)

    def value_tile(j, slot, g, alpha):
        vb = vaug_scr[:, pl.ds(pl.multiple_of(j * blk, blk), blk)]
        acc_scr[:, groups[g]] = alpha * acc_scr[:, groups[g]] + _dot(vb, p_scr[slot, :, groups[g]])

    def past_step(j, slot, m, mx, a_prev):
        mx_next, m_alpha = [], []
        for g in range(ng):
            value_tile(jnp.maximum(j - 1, 0), 1 - slot, g, a_prev[g])
            mx_next.append(score_tile(j + 1, 1 - slot, g, False))
            m_alpha.append(softmax_tile(j, slot, g, m[g], mx[g]))
        mx_next = tuple(mx_next)
        return tuple(ma[0] for ma in m_alpha), mx_next, tuple(ma[1] for ma in m_alpha)

    def body(jj, carry):
        for u in range(ATTN_UNROLL):
            carry = past_step(ATTN_UNROLL * jj + u, u % 2, *carry)
        return carry

    acc_scr[...] = jnp.zeros_like(acc_scr)
    p_scr[1] = jnp.zeros(p_scr.shape[1:], p_scr.dtype)
    init = (tuple(jnp.full((1, blk), NEG, jnp.float32) for _ in range(ng)),
            tuple(score_tile(0, 0, g, False) for g in range(ng)),
            tuple(jnp.ones((1, blk), jnp.float32) for _ in range(ng)))
    m, mx, alpha = lax.fori_loop(0, n_past // ATTN_UNROLL, body, init)
    m, mx, alpha = list(m), list(mx), list(alpha)

    for g in range(ng):
        value_tile(jnp.maximum(n_past - 1, 0), 1, g, alpha[g])
    mxs = {(g, 0): mx[g] for g in range(1, ng)}
    mxs[(0, 0)] = score_tile(n_past, 0, 0, True)
    for t in range(1, ng):
        for g in range(t, ng):
            mxs[(g, t)] = score_tile(n_past + t, t, g, g == t)
    span0 = pl.multiple_of(n_past * blk, blk)
    for g in range(ng):
        bs = [bias_ref[pl.ds(n_past + t, 1), groups[g]] for t in range(g + 1)]
        m_new = m[g]
        for t in range(g + 1):
            m_new = jnp.maximum(m_new, mxs[(g, t)] + bs[t])
        for t in range(g + 1):
            shift = jnp.maximum(m_new - bs[t], mxs[(g, t)])
            ptail_scr[t * blk:(t + 1) * blk, groups[g]] = (
                jnp.exp2(s_scr[t, :, groups[g]] - shift).astype(ptail_scr.dtype))
        pv = _dot(vaug_scr[:, pl.ds(span0, (g + 1) * blk)], ptail_scr[:(g + 1) * blk, groups[g]])
        acc_scr[:, groups[g]] = jnp.exp2(m[g] - m_new) * acc_scr[:, groups[g]] + pv
    o_ref[...] = (acc_scr[:HEAD_DIM, :] / acc_scr[HEAD_DIM:HEAD_DIM + 1, :]).T.astype(o_ref.dtype)


def _attention(proj, bias):
    s = proj.shape[0]
    tq = ATTN_QB * MOBA_BLOCK
    kcol, vcol = K_COL // HEAD_DIM, V_COL // HEAD_DIM
    return pl.pallas_call(
        _attn_kernel,
        out_shape=jax.ShapeDtypeStruct((s, ATTN_W), jnp.bfloat16),
        grid=(N_HEADS, s // tq),
        in_specs=[pl.BlockSpec((tq, HEAD_DIM), lambda h, i: (i, h)),
                  pl.BlockSpec((None, s // MOBA_BLOCK, tq), lambda h, i: (h, 0, i)),
                  pl.BlockSpec((s, HEAD_DIM), lambda h, i: (0, kcol + h)),
                  pl.BlockSpec((s, HEAD_DIM), lambda h, i: (0, vcol + h))],
        out_specs=pl.BlockSpec((tq, HEAD_DIM), lambda h, i: (i, h)),
        scratch_shapes=[pltpu.VMEM((HEAD_DIM + ONES_ROWS, s), jnp.bfloat16),
                        pltpu.VMEM((HEAD_DIM + ONES_ROWS, tq), jnp.float32),
                        pltpu.VMEM((ATTN_QB, MOBA_BLOCK, tq), jnp.float32),
                        pltpu.VMEM((2, MOBA_BLOCK, tq), jnp.bfloat16),
                        pltpu.VMEM((tq, tq), jnp.bfloat16)],
        compiler_params=_params(("arbitrary", "arbitrary"), 56),
        name="attention",
    )(proj, bias, proj, proj)


POOL_HALO = 16


def _pool_kernel(u_ref, halo_ref, w_ref, ls_ref, o_ref):
    i = pl.program_id(0)
    tm = u_ref.shape[0]
    u = u_ref[...].astype(jnp.float32)
    halo = jnp.where(i == 0, 0.0, halo_ref[...].astype(jnp.float32))
    t = i * tm + lax.broadcasted_iota(jnp.int32, (tm, 1), 0)
    for g, win in enumerate(POOL_WINDOWS):
        cols = slice(g * POOL_GW, (g + 1) * POOL_GW)
        ug = u[:, cols]
        e = jnp.concatenate([halo[:, cols], ug], axis=0)
        span = 1
        while span < win:
            e = e + pltpu.roll(e, span, axis=0)
            span *= 2
        cnt = jnp.minimum(t + 1, win).astype(jnp.float32)
        dg = e[POOL_HALO:, :] / cnt - ug
        mixed = _dot(dg.astype(jnp.bfloat16), w_ref[g])
        o_ref[:, cols] = (mixed * ls_ref[:, cols]).astype(o_ref.dtype)


def _pool(proj, w_grp, ls):
    s = proj.shape[0]
    tm = 1024
    ucol = U_COL // POOL_W
    return pl.pallas_call(
        _pool_kernel,
        out_shape=jax.ShapeDtypeStruct((s, POOL_W), jnp.bfloat16),
        grid=(s // tm,),
        in_specs=[pl.BlockSpec((tm, POOL_W), lambda i: (i, ucol)),
                  pl.BlockSpec((POOL_HALO, POOL_W),
                               lambda i: (jnp.maximum(i * (tm // POOL_HALO) - 1, 0), ucol)),
                  pl.BlockSpec(w_grp.shape, lambda i: (0, 0, 0)),
                  pl.BlockSpec((1, POOL_W), lambda i: (0, 0))],
        out_specs=pl.BlockSpec((tm, POOL_W), lambda i: (i, 0)),
        compiler_params=_params(("arbitrary",), 48),
        name="pool",
    )(proj, proj, w_grp, ls)


def _merge_kernel(a_ref, p_ref, ga_ref, gp_ref, x_ref, g1_ref, wa_ref, wp_ref, wo_ref, o_ref):
    j = pl.program_id(1)

    @pl.when(j == 0)
    def _():
        o_ref[...] = jnp.zeros_like(o_ref)

    ya = _dot(a_ref[...], wa_ref[...])
    yp = _dot(p_ref[...], wp_ref[...])
    merged = ga_ref[...].astype(jnp.float32) * ya + gp_ref[...].astype(jnp.float32) * yp
    o_ref[...] += _dot(merged.astype(jnp.bfloat16), wo_ref[...])

    @pl.when(j == pl.num_programs(1) - 1)
    def _():
        o_ref[...] = x_ref[...] + g1_ref[...] * o_ref[...]


def _merge(attn, pooled, proj, x2, g1, wa, wp, wo):
    s, d = x2.shape
    tm, tn = 512, 512
    ga0, gp0 = G_COL // tn, (G_COL + d) // tn
    return pl.pallas_call(
        _merge_kernel,
        out_shape=jax.ShapeDtypeStruct((s, d), jnp.float32),
        grid=(s // tm, d // tn),
        in_specs=[pl.BlockSpec((tm, ATTN_W), lambda i, j: (i, 0)),
                  pl.BlockSpec((tm, POOL_W), lambda i, j: (i, 0)),
                  pl.BlockSpec((tm, tn), lambda i, j: (i, ga0 + j)),
                  pl.BlockSpec((tm, tn), lambda i, j: (i, gp0 + j)),
                  pl.BlockSpec((tm, d), lambda i, j: (i, 0)),
                  pl.BlockSpec((1, d), lambda i, j: (0, 0)),
                  pl.BlockSpec((ATTN_W, tn), lambda i, j: (0, j)),
                  pl.BlockSpec((POOL_W, tn), lambda i, j: (0, j)),
                  pl.BlockSpec((tn, d), lambda i, j: (j, 0))],
        out_specs=pl.BlockSpec((tm, d), lambda i, j: (i, 0)),
        compiler_params=_params(("arbitrary", "arbitrary"), 48),
        name="merge",
    )(attn, pooled, proj, proj, x2, g1, wa, wp, wo)


FFN_CHUNK = 256


def _ffn_kernel(x_ref, g_ref, sh_ref, sc_ref, g2_ref, wua_ref, wub_ref, cw_ref, cb_ref, wdp_ref, wd_ref,
                o_ref, h_scr, tail_scr, upa_scr, upb_scr):
    i = pl.program_id(0)
    j = pl.program_id(1)
    tm = x_ref.shape[0]
    n_chunks = cw_ref.shape[0] // 2

    @pl.when(j == 0)
    def _():
        h_scr[...] = _norm_modulate(x_ref[...], g_ref[...], sh_ref[...], sc_ref[...]).astype(h_scr.dtype)
        o_ref[...] = jnp.zeros_like(o_ref)

    @pl.when(i == 0)
    def _():
        tail_scr[j] = jnp.zeros(tail_scr.shape[1:], tail_scr.dtype)

    chunks = [slice(c * FFN_CHUNK, (c + 1) * FFN_CHUNK) for c in range(2)]

    def up_stage(c):
        for half, (w_ref, u_scr) in enumerate(((wua_ref, upa_scr), (wub_ref, upb_scr))):
            u_scr[c, :SUBLANES, :] = tail_scr[j, half, :, chunks[c]]
            up = _dot(h_scr[...], w_ref[:, chunks[c]])
            u_scr[c, SUBLANES:, :] = up
            tail_scr[j, half, :, chunks[c]] = up[tm - SUBLANES:, :]

    def conv(ext, n):
        cw = cw_ref[n]
        out = cb_ref[n] + cw[CONV_W - 1:CONV_W, :] * ext[SUBLANES:, :]
        for back in range(1, CONV_W):
            tap = CONV_W - 1 - back
            out = out + cw[tap:tap + 1, :] * pltpu.roll(ext, back, axis=0)[SUBLANES:, :]
        return out

    def gated(c, n):
        a = conv(upa_scr[c], n)
        b = conv(upb_scr[c], n_chunks + n)
        return (a * _sigmoid(a) * b).astype(jnp.bfloat16)

    up_stage(0)
    act_prev = gated(1, jnp.maximum(2 * j - 1, 0))
    act_prev = jnp.where(j > 0, act_prev, jnp.zeros_like(act_prev))
    up_stage(1)
    act_0 = gated(0, 2 * j)
    o_ref[...] += _dot(act_prev, wdp_ref[chunks[1], :]) + _dot(act_0, wd_ref[chunks[0], :])

    @pl.when(j == pl.num_programs(1) - 1)
    def _():
        o_ref[...] += _dot(gated(1, 2 * j + 1), wd_ref[chunks[1], :])
        o_ref[...] = x_ref[...] + g2_ref[...] * o_ref[...]


def _ffn(x1, g, sh, sc, g2, w_up, conv_w, conv_b, w_down):
    s, d = x1.shape
    tm, tf = 512, 2 * FFN_CHUNK
    nf = D_FF // tf
    n_chunks = 2 * D_FF // FFN_CHUNK
    cw = conv_w.reshape(CONV_W, n_chunks, FFN_CHUNK).transpose(1, 0, 2)
    cb = conv_b.reshape(n_chunks, 1, FFN_CHUNK)
    vec = lambda: pl.BlockSpec((1, d), lambda i, j: (0, 0))
    return pl.pallas_call(
        _ffn_kernel,
        out_shape=jax.ShapeDtypeStruct((s, d), jnp.float32),
        grid=(s // tm, nf),
        in_specs=[pl.BlockSpec((tm, d), lambda i, j: (i, 0)),
                  vec(), vec(), vec(), vec(),
                  pl.BlockSpec((d, tf), lambda i, j: (0, j)),
                  pl.BlockSpec((d, tf), lambda i, j: (0, nf + j)),
                  pl.BlockSpec(cw.shape, lambda i, j: (0, 0, 0)),
                  pl.BlockSpec(cb.shape, lambda i, j: (0, 0, 0)),
                  pl.BlockSpec((tf, d), lambda i, j: (jnp.maximum(j - 1, 0), 0)),
                  pl.BlockSpec((tf, d), lambda i, j: (j, 0))],
        out_specs=pl.BlockSpec((tm, d), lambda i, j: (i, 0)),
        scratch_shapes=[pltpu.VMEM((tm, d), jnp.bfloat16),
                        pltpu.VMEM((nf, 2, SUBLANES, tf), jnp.float32),
                        pltpu.VMEM((2, SUBLANES + tm, FFN_CHUNK), jnp.float32),
                        pltpu.VMEM((2, SUBLANES + tm, FFN_CHUNK), jnp.float32)],
        compiler_params=_params(("arbitrary", "arbitrary"), 48),
        name="ffn",
    )(x1, g, sh, sc, g2, w_up, w_up, cw, cb, w_down, w_down)


def _rope_tables(s):
    half = HEAD_DIM // 2
    inv = ROPE_THETA ** (-jnp.arange(half, dtype=jnp.float32) / half)
    ang = jnp.arange(s).astype(jnp.float32)[:, None] * inv[None, :]
    cos, sin = jnp.cos(ang), jnp.sin(ang)
    return jnp.concatenate([cos, cos], axis=1), jnp.concatenate([-sin, sin], axis=1)


def kernel(x, c, w_ada, b_ada, norm_mix_g, w_in, q_norm_g, k_norm_g, w_pool_grp, pool_scale, w_attn_br,
           w_pool_br, w_gate, b_gate, w_o, norm_ffn_g, w_up, conv_w, conv_b, w_down):
    batch, s, d = x.shape
    assert batch == 1 and d == D_MODEL and w_ada.shape[0] == 1
    bf = jnp.bfloat16
    row = lambda v: v.reshape(1, -1)
    cos_t, sin_t = _rope_tables(s)

    mod = _ada(c.reshape(d, 1), w_ada[0], row(b_ada[0]))
    sh1, sc1, g1, sh2, sc2, g2 = [mod[:, n * d:(n + 1) * d] for n in range(6)]

    x2 = x[0]
    proj = _proj(x2, row(norm_mix_g[0]), sh1, sc1, w_in[0].astype(bf), w_gate[0].astype(bf), row(b_gate[0]),
                 cos_t, sin_t, row(q_norm_g[0]), row(k_norm_g[0]))
    bias = _select(proj)
    attn = _attention(proj, bias)
    pooled = _pool(proj, w_pool_grp[0].astype(bf), row(pool_scale[0]))
    x1 = _merge(attn, pooled, proj, x2, g1, w_attn_br[0].astype(bf), w_pool_br[0].astype(bf),
                w_o[0].astype(bf))
    out = _ffn(x1, row(norm_ffn_g[0]), sh2, sc2, g2, w_up[0].astype(bf), conv_w[0], conv_b[0],
               w_down[0].astype(bf))
    return out[None]
```

```python
import functools
import math

import jax
import jax.numpy as jnp
from jax import lax
from jax.experimental import pallas as pl
from jax.experimental.pallas import tpu as pltpu

D_MODEL = 2048
N_HEADS = 8
HEAD_DIM = 128
ATTN_W = N_HEADS * HEAD_DIM
MOBA_BLOCK = 256
MOBA_TOPK = 3
POOL_WINDOWS = (2, 4, 8, 16)
POOL_W = 1024
POOL_GW = POOL_W // len(POOL_WINDOWS)
D_FF = 5632
CONV_W = 3
ROPE_THETA = 10000.0
EPS = 1e-6
NEG = -1e30
BLOCK_SHIFT = MOBA_BLOCK.bit_length() - 1
LOG2E = math.log2(math.e)

LANES = 128
SUBLANES = 8
PROJ_W = 4 * ATTN_W + 2 * D_MODEL
Q_COL, K_COL, V_COL, U_COL, G_COL = 0, ATTN_W, 2 * ATTN_W, 3 * ATTN_W, 4 * ATTN_W

MIB = 1024 * 1024


def _dot(a, b):
    return jnp.dot(a, b, preferred_element_type=jnp.float32)


def _dot_nt(a, b):
    return lax.dot_general(a, b, (((1,), (1,)), ((), ())), preferred_element_type=jnp.float32)


def _sigmoid(x):
    return 0.5 * jnp.tanh(0.5 * x) + 0.5


def _params(semantics, vmem_mib):
    return pltpu.CompilerParams(dimension_semantics=semantics, vmem_limit_bytes=vmem_mib * MIB)


NORM_ROWS = 64


def _norm_modulate_into(h_scr, x_ref, g_ref, shift_ref, scale_ref):
    gain = g_ref[...] * (1.0 + scale_ref[...])
    shift = shift_ref[...]

    def body(r, carry):
        rows = pl.ds(pl.multiple_of(r * NORM_ROWS, NORM_ROWS), NORM_ROWS)
        x = x_ref[rows, :]
        inv = lax.rsqrt(jnp.mean(x * x, axis=-1, keepdims=True) + EPS)
        h_scr[rows, :] = (x * inv * gain + shift).astype(h_scr.dtype)
        return carry

    lax.fori_loop(0, x_ref.shape[0] // NORM_ROWS, body, 0, unroll=4)


def _ada_kernel(c_ref, w_ref, b_ref, o_ref):
    c = c_ref[...]
    s = c * jax.nn.sigmoid(c)
    o_ref[...] = jnp.sum(s * w_ref[...], axis=0, keepdims=True) + b_ref[...]


def _ada(c_col, w_ada, b_ada):
    d, n = w_ada.shape
    tn = 1024
    return pl.pallas_call(
        _ada_kernel,
        out_shape=jax.ShapeDtypeStruct((1, n), jnp.float32),
        grid=(n // tn,),
        in_specs=[pl.BlockSpec((d, 1), lambda j: (0, 0)),
                  pl.BlockSpec((d, tn), lambda j: (0, j)),
                  pl.BlockSpec((1, tn), lambda j: (0, j))],
        out_specs=pl.BlockSpec((1, tn), lambda j: (0, j)),
        compiler_params=_params(("arbitrary",), 40),
        name="ada",
    )(c_col, w_ada, b_ada)


def _proj_kernel(x_ref, g_ref, sh_ref, sc_ref, wi_ref, wg_ref, cb_ref, cos_ref, sin_ref, qg_ref, kg_ref,
                 o_ref, h_scr, acc_scr, *, tn):
    j = pl.program_id(1)

    @pl.when(j == 0)
    def _():
        _norm_modulate_into(h_scr, x_ref, g_ref, sh_ref, sc_ref)

    @pl.when(j < G_COL // tn)
    def _():
        acc_scr[...] = _dot(h_scr[...], wi_ref[...])

    @pl.when(j >= G_COL // tn)
    def _():
        acc_scr[...] = _dot(h_scr[...], wg_ref[...])

    def head_norm_rope(gain_ref, out_scale):
        for hh in range(tn // HEAD_DIM):
            t = acc_scr[:, hh * HEAD_DIM:(hh + 1) * HEAD_DIM]
            ms = jnp.mean(t * t, axis=-1, keepdims=True)
            y = t * lax.rsqrt(ms + EPS) * gain_ref[...]
            r = y * cos_ref[...] + pltpu.roll(y, HEAD_DIM // 2, axis=1) * sin_ref[...]
            if out_scale != 1.0:
                r = r * out_scale
            o_ref[:, hh * HEAD_DIM:(hh + 1) * HEAD_DIM] = r.astype(o_ref.dtype)

    @pl.when(j < K_COL // tn)
    def _():
        head_norm_rope(qg_ref, HEAD_DIM ** -0.5 * LOG2E)

    @pl.when(jnp.logical_and(j >= K_COL // tn, j < V_COL // tn))
    def _():
        head_norm_rope(kg_ref, 1.0)

    @pl.when(jnp.logical_and(j >= V_COL // tn, j < G_COL // tn))
    def _():
        o_ref[...] = acc_scr[...].astype(o_ref.dtype)

    @pl.when(j >= G_COL // tn)
    def _():
        o_ref[...] = _sigmoid(acc_scr[...] + cb_ref[...]).astype(o_ref.dtype)


def _proj(x2, g, sh, sc, w_in, w_gate, b_gate, cos_t, sin_t, qg, kg):
    s, d = x2.shape
    tm, tn = 1024, 512
    n_in = G_COL // tn
    vec = lambda: pl.BlockSpec((1, d), lambda i, j: (0, 0))
    return pl.pallas_call(
        functools.partial(_proj_kernel, tn=tn),
        out_shape=jax.ShapeDtypeStruct((s, PROJ_W), jnp.bfloat16),
        grid=(s // tm, PROJ_W // tn),
        in_specs=[pl.BlockSpec((tm, d), lambda i, j: (i, 0)),
                  vec(), vec(), vec(),
                  pl.BlockSpec((d, tn), lambda i, j: (0, jnp.minimum(j, n_in - 1))),
                  pl.BlockSpec((d, tn), lambda i, j: (0, jnp.maximum(j - n_in, 0))),
                  pl.BlockSpec((1, tn), lambda i, j: (0, jnp.maximum(j - n_in, 0))),
                  pl.BlockSpec((tm, HEAD_DIM), lambda i, j: (i, 0)),
                  pl.BlockSpec((tm, HEAD_DIM), lambda i, j: (i, 0)),
                  pl.BlockSpec((1, HEAD_DIM), lambda i, j: (0, 0)),
                  pl.BlockSpec((1, HEAD_DIM), lambda i, j: (0, 0))],
        out_specs=pl.BlockSpec((tm, tn), lambda i, j: (i, j)),
        scratch_shapes=[pltpu.VMEM((tm, d), jnp.bfloat16),
                        pltpu.VMEM((tm, tn), jnp.float32)],
        compiler_params=_params(("arbitrary", "arbitrary"), 48),
        name="proj",
    )(x2, g, sh, sc, w_in, w_gate, b_gate, cos_t, sin_t, qg, kg)


def _select_kernel(q_ref, k_ref, bias_ref, km_scr, *, tq):
    i = pl.program_id(1)
    nb = k_ref.shape[0] // MOBA_BLOCK

    @pl.when(i == 0)
    def _():
        k = k_ref[...].astype(jnp.float32).reshape(nb, MOBA_BLOCK, HEAD_DIM)
        km = jnp.sum(k, axis=1) * (1.0 / MOBA_BLOCK)
        p0 = km.astype(jnp.bfloat16)
        r1 = km - p0.astype(jnp.float32)
        p1 = r1.astype(jnp.bfloat16)
        p2 = (r1 - p1.astype(jnp.float32)).astype(jnp.bfloat16)
        km_scr[0] = p0
        km_scr[1] = p1
        km_scr[2] = p2

    q = q_ref[...]
    gate = _dot_nt(km_scr[0], q) + _dot_nt(km_scr[1], q) + _dot_nt(km_scr[2], q)
    blk = lax.broadcasted_iota(jnp.int32, gate.shape, 0)
    qpos = i * tq + lax.broadcasted_iota(jnp.int32, gate.shape, 1)
    own = lax.shift_right_logical(qpos, BLOCK_SHIFT)
    past = blk < own
    g = jnp.where(past, gate, NEG)
    bias = jnp.where(blk == own, 0.0, NEG)
    for _ in range(MOBA_TOPK):
        top = jnp.max(g, axis=0, keepdims=True)
        first = jnp.min(jnp.where(g == top, blk, nb), axis=0, keepdims=True)
        pick = blk == first
        bias = jnp.where(jnp.logical_and(pick, past), 0.0, bias)
        g = jnp.where(pick, -jnp.inf, g)
    bias_ref[...] = bias


def _eye(n):
    return (lax.broadcasted_iota(jnp.int32, (n, n), 0)
            == lax.broadcasted_iota(jnp.int32, (n, n), 1)).astype(jnp.bfloat16)


def _select(proj):
    s = proj.shape[0]
    tq = 2048
    kcol = K_COL // HEAD_DIM
    return pl.pallas_call(
        functools.partial(_select_kernel, tq=tq),
        out_shape=jax.ShapeDtypeStruct((N_HEADS, s // MOBA_BLOCK, s), jnp.float32),
        grid=(N_HEADS, s // tq),
        in_specs=[pl.BlockSpec((tq, HEAD_DIM), lambda h, i: (i, h)),
                  pl.BlockSpec((s, HEAD_DIM), lambda h, i: (0, kcol + h))],
        out_specs=pl.BlockSpec((None, s // MOBA_BLOCK, tq), lambda h, i: (h, 0, i)),
        scratch_shapes=[pltpu.VMEM((3, s // MOBA_BLOCK, HEAD_DIM), jnp.bfloat16)],
        compiler_params=_params(("arbitrary", "arbitrary"), 48),
        name="select",
    )(proj, proj)


ATTN_QB = 8
ATTN_UNROLL = 4
ONES_ROWS = 16
XPOSE_CHUNK = 2048


def _attn_kernel(q_ref, bias_ref, k_ref, v_ref, o_ref, vaug_scr, acc_scr, s_scr, p_scr, ptail_scr):
    i = pl.program_id(1)
    s_len = k_ref.shape[0]
    blk = MOBA_BLOCK

    @pl.when(i == 0)
    def _():
        for c in range(s_len // XPOSE_CHUNK):
            keys = slice(c * XPOSE_CHUNK, (c + 1) * XPOSE_CHUNK)
            vaug_scr[:HEAD_DIM, keys] = _dot_nt(_eye(HEAD_DIM), v_ref[keys, :]).astype(vaug_scr.dtype)
        vaug_scr[HEAD_DIM:, :] = jnp.ones((ONES_ROWS, s_len), vaug_scr.dtype)

    ng = ATTN_QB
    n_past = i * ng
    groups = [slice(g * blk, (g + 1) * blk) for g in range(ng)]
    key_le_query = (lax.broadcasted_iota(jnp.int32, (blk, blk), 0)
                    <= lax.broadcasted_iota(jnp.int32, (blk, blk), 1))

    def score_tile(j, slot, g, causal):
        kb = k_ref[pl.ds(pl.multiple_of(j * blk, blk), blk), :]
        s_t = _dot_nt(kb, q_ref[groups[g], :])
        if causal:
            s_t = jnp.where(key_le_query, s_t, NEG)
        s_scr[slot, :, groups[g]] = s_t
        return jnp.max(s_t, axis=0, keepdims=True)

    def softmax_tile(j, slot, g, m_old, mx):
        b = bias_ref[pl.ds(j, 1), groups[g]]
        m_new = jnp.maximum(m_old, mx + b)
        shift = jnp.maximum(m_new - b, mx)
        p_scr[slot, :, groups[g]] = jnp.exp2(s_scr[slot, :, groups[g]] - shift).astype(p_scr.dtype)
        return m_new, jnp.exp2(m_old - m_new)

    def value_tile(j, slot, g, alpha):
        vb = vaug_scr[:, pl.ds(pl.multiple_of(j * blk, blk), blk)]
        acc_scr[:, groups[g]] = alpha * acc_scr[:, groups[g]] + _dot(vb, p_scr[slot, :, groups[g]])

    def past_step(j, slot, m, mx, a_prev):
        mx_next, m_alpha = [], []
        for g in range(ng):
            value_tile(jnp.maximum(j - 1, 0), 1 - slot, g, a_prev[g])
            mx_next.append(score_tile(j + 1, 1 - slot, g, False))
            m_alpha.append(softmax_tile(j, slot, g, m[g], mx[g]))
        mx_next = tuple(mx_next)
        return tuple(ma[0] for ma in m_alpha), mx_next, tuple(ma[1] for ma in m_alpha)

    def body(jj, carry):
        for u in range(ATTN_UNROLL):
            carry = past_step(ATTN_UNROLL * jj + u, u % 2, *carry)
        return carry

    acc_scr[...] = jnp.zeros_like(acc_scr)
    p_scr[1] = jnp.zeros(p_scr.shape[1:], p_scr.dtype)
    init = (tuple(jnp.full((1, blk), NEG, jnp.float32) for _ in range(ng)),
            tuple(score_tile(0, 0, g, False) for g in range(ng)),
            tuple(jnp.ones((1, blk), jnp.float32) for _ in range(ng)))
    m, mx, alpha = lax.fori_loop(0, n_past // ATTN_UNROLL, body, init)
    m, mx, alpha = list(m), list(mx), list(alpha)

    for g in range(ng):
        value_tile(jnp.maximum(n_past - 1, 0), 1, g, alpha[g])
    mxs = {(g, 0): mx[g] for g in range(1, ng)}
    mxs[(0, 0)] = score_tile(n_past, 0, 0, True)
    for t in range(1, ng):
        for g in range(t, ng):
            mxs[(g, t)] = score_tile(n_past + t, t, g, g == t)
    span0 = pl.multiple_of(n_past * blk, blk)
    for g in range(ng):
        bs = [bias_ref[pl.ds(n_past + t, 1), groups[g]] for t in range(g + 1)]
        m_new = m[g]
        for t in range(g + 1):
            m_new = jnp.maximum(m_new, mxs[(g, t)] + bs[t])
        for t in range(g + 1):
            shift = jnp.maximum(m_new - bs[t], mxs[(g, t)])
            ptail_scr[t * blk:(t + 1) * blk, groups[g]] = (
                jnp.exp2(s_scr[t, :, groups[g]] - shift).astype(ptail_scr.dtype))
        pv = _dot(vaug_scr[:, pl.ds(span0, (g + 1) * blk)], ptail_scr[:(g + 1) * blk, groups[g]])
        acc_scr[:, groups[g]] = jnp.exp2(m[g] - m_new) * acc_scr[:, groups[g]] + pv
    o_ref[...] = (acc_scr[:HEAD_DIM, :] / acc_scr[HEAD_DIM:HEAD_DIM + 1, :]).T.astype(o_ref.dtype)


def _attention(proj, bias):
    s = proj.shape[0]
    tq = ATTN_QB * MOBA_BLOCK
    kcol, vcol = K_COL // HEAD_DIM, V_COL // HEAD_DIM
    return pl.pallas_call(
        _attn_kernel,
        out_shape=jax.ShapeDtypeStruct((s, ATTN_W), jnp.bfloat16),
        grid=(N_HEADS, s // tq),
        in_specs=[pl.BlockSpec((tq, HEAD_DIM), lambda h, i: (i, h)),
                  pl.BlockSpec((None, s // MOBA_BLOCK, tq), lambda h, i: (h, 0, i)),
                  pl.BlockSpec((s, HEAD_DIM), lambda h, i: (0, kcol + h)),
                  pl.BlockSpec((s, HEAD_DIM), lambda h, i: (0, vcol + h))],
        out_specs=pl.BlockSpec((tq, HEAD_DIM), lambda h, i: (i, h)),
        scratch_shapes=[pltpu.VMEM((HEAD_DIM + ONES_ROWS, s), jnp.bfloat16),
                        pltpu.VMEM((HEAD_DIM + ONES_ROWS, tq), jnp.float32),
                        pltpu.VMEM((ATTN_QB, MOBA_BLOCK, tq), jnp.float32),
                        pltpu.VMEM((2, MOBA_BLOCK, tq), jnp.bfloat16),
                        pltpu.VMEM((tq, tq), jnp.bfloat16)],
        compiler_params=_params(("arbitrary", "arbitrary"), 56),
        name="attention",
    )(proj, bias, proj, proj)


POOL_HALO = 16


def _pool_kernel(u_ref, halo_ref, w_ref, ls_ref, o_ref):
    i = pl.program_id(0)
    tm = u_ref.shape[0]
    u = u_ref[...].astype(jnp.float32)
    halo = jnp.where(i == 0, 0.0, halo_ref[...].astype(jnp.float32))
    t = i * tm + lax.broadcasted_iota(jnp.int32, (tm, 1), 0)
    for g, win in enumerate(POOL_WINDOWS):
        cols = slice(g * POOL_GW, (g + 1) * POOL_GW)
        ug = u[:, cols]
        e = jnp.concatenate([halo[:, cols], ug], axis=0)
        span = 1
        while span < win:
            e = e + pltpu.roll(e, span, axis=0)
            span *= 2
        cnt = jnp.minimum(t + 1, win).astype(jnp.float32)
        dg = e[POOL_HALO:, :] / cnt - ug
        mixed = _dot(dg.astype(jnp.bfloat16), w_ref[g])
        o_ref[:, cols] = (mixed * ls_ref[:, cols]).astype(o_ref.dtype)


def _pool(proj, w_grp, ls):
    s = proj.shape[0]
    tm = 1024
    ucol = U_COL // POOL_W
    return pl.pallas_call(
        _pool_kernel,
        out_shape=jax.ShapeDtypeStruct((s, POOL_W), jnp.bfloat16),
        grid=(s // tm,),
        in_specs=[pl.BlockSpec((tm, POOL_W), lambda i: (i, ucol)),
                  pl.BlockSpec((POOL_HALO, POOL_W),
                               lambda i: (jnp.maximum(i * (tm // POOL_HALO) - 1, 0), ucol)),
                  pl.BlockSpec(w_grp.shape, lambda i: (0, 0, 0)),
                  pl.BlockSpec((1, POOL_W), lambda i: (0, 0))],
        out_specs=pl.BlockSpec((tm, POOL_W), lambda i: (i, 0)),
        compiler_params=_params(("arbitrary",), 48),
        name="pool",
    )(proj, proj, w_grp, ls)


def _merge_kernel(a_ref, p_ref, ga_ref, gp_ref, x_ref, g1_ref, wa_ref, wp_ref, wo_ref, o_ref):
    j = pl.program_id(1)

    @pl.when(j == 0)
    def _():
        o_ref[...] = jnp.zeros_like(o_ref)

    ya = _dot(a_ref[...], wa_ref[...])
    yp = _dot(p_ref[...], wp_ref[...])
    merged = ga_ref[...].astype(jnp.float32) * ya + gp_ref[...].astype(jnp.float32) * yp
    o_ref[...] += _dot(merged.astype(jnp.bfloat16), wo_ref[...])

    @pl.when(j == pl.num_programs(1) - 1)
    def _():
        o_ref[...] = x_ref[...] + g1_ref[...] * o_ref[...]


def _merge(attn, pooled, proj, x2, g1, wa, wp, wo):
    s, d = x2.shape
    tm, tn = 512, 512
    ga0, gp0 = G_COL // tn, (G_COL + d) // tn
    return pl.pallas_call(
        _merge_kernel,
        out_shape=jax.ShapeDtypeStruct((s, d), jnp.float32),
        grid=(s // tm, d // tn),
        in_specs=[pl.BlockSpec((tm, ATTN_W), lambda i, j: (i, 0)),
                  pl.BlockSpec((tm, POOL_W), lambda i, j: (i, 0)),
                  pl.BlockSpec((tm, tn), lambda i, j: (i, ga0 + j)),
                  pl.BlockSpec((tm, tn), lambda i, j: (i, gp0 + j)),
                  pl.BlockSpec((tm, d), lambda i, j: (i, 0)),
                  pl.BlockSpec((1, d), lambda i, j: (0, 0)),
                  pl.BlockSpec((ATTN_W, tn), lambda i, j: (0, j)),
                  pl.BlockSpec((POOL_W, tn), lambda i, j: (0, j)),
                  pl.BlockSpec((tn, d), lambda i, j: (j, 0))],
        out_specs=pl.BlockSpec((tm, d), lambda i, j: (i, 0)),
        compiler_params=_params(("arbitrary", "arbitrary"), 48),
        name="merge",
    )(attn, pooled, proj, proj, x2, g1, wa, wp, wo)


FFN_CHUNK = 256


def _ffn_kernel(x_ref, g_ref, sh_ref, sc_ref, g2_ref, wua_ref, wub_ref, cw_ref, cb_ref, wdp_ref, wd_ref,
                o_ref, h_scr, tail_scr, upa_scr, upb_scr):
    i = pl.program_id(0)
    j = pl.program_id(1)
    tm = x_ref.shape[0]
    n_chunks = cw_ref.shape[0] // 2

    @pl.when(j == 0)
    def _():
        _norm_modulate_into(h_scr, x_ref, g_ref, sh_ref, sc_ref)
        o_ref[...] = jnp.zeros_like(o_ref)

    @pl.when(i == 0)
    def _():
        tail_scr[j] = jnp.zeros(tail_scr.shape[1:], tail_scr.dtype)

    chunks = [slice(c * FFN_CHUNK, (c + 1) * FFN_CHUNK) for c in range(2)]

    def up_stage(c):
        for half, (w_ref, u_scr) in enumerate(((wua_ref, upa_scr), (wub_ref, upb_scr))):
            u_scr[c, :SUBLANES, :] = tail_scr[j, half, :, chunks[c]]
            up = _dot(h_scr[...], w_ref[:, chunks[c]])
            u_scr[c, SUBLANES:, :] = up
            tail_scr[j, half, :, chunks[c]] = up[tm - SUBLANES:, :]

    def conv(ext, n):
        cw = cw_ref[n]
        out = cb_ref[n] + cw[CONV_W - 1:CONV_W, :] * ext[SUBLANES:, :]
        for back in range(1, CONV_W):
            tap = CONV_W - 1 - back
            out = out + cw[tap:tap + 1, :] * pltpu.roll(ext, back, axis=0)[SUBLANES:, :]
        return out

    def gated(c, n):
        a = conv(upa_scr[c], n)
        b = conv(upb_scr[c], n_chunks + n)
        return (a * _sigmoid(a) * b).astype(jnp.bfloat16)

    up_stage(0)
    act_prev = gated(1, jnp.maximum(2 * j - 1, 0))
    act_prev = jnp.where(j > 0, act_prev, jnp.zeros_like(act_prev))
    up_stage(1)
    act_0 = gated(0, 2 * j)
    o_ref[...] += _dot(act_prev, wdp_ref[chunks[1], :]) + _dot(act_0, wd_ref[chunks[0], :])

    @pl.when(j == pl.num_programs(1) - 1)
    def _():
        o_ref[...] += _dot(gated(1, 2 * j + 1), wd_ref[chunks[1], :])
        o_ref[...] = x_ref[...] + g2_ref[...] * o_ref[...]


def _ffn(x1, g, sh, sc, g2, w_up, conv_w, conv_b, w_down):
    s, d = x1.shape
    tm, tf = 512, 2 * FFN_CHUNK
    nf = D_FF // tf
    n_chunks = 2 * D_FF // FFN_CHUNK
    cw = conv_w.reshape(CONV_W, n_chunks, FFN_CHUNK).transpose(1, 0, 2)
    cb = conv_b.reshape(n_chunks, 1, FFN_CHUNK)
    vec = lambda: pl.BlockSpec((1, d), lambda i, j: (0, 0))
    return pl.pallas_call(
        _ffn_kernel,
        out_shape=jax.ShapeDtypeStruct((s, d), jnp.float32),
        grid=(s // tm, nf),
        in_specs=[pl.BlockSpec((tm, d), lambda i, j: (i, 0)),
                  vec(), vec(), vec(), vec(),
                  pl.BlockSpec((d, tf), lambda i, j: (0, j)),
                  pl.BlockSpec((d, tf), lambda i, j: (0, nf + j)),
                  pl.BlockSpec(cw.shape, lambda i, j: (0, 0, 0)),
                  pl.BlockSpec(cb.shape, lambda i, j: (0, 0, 0)),
                  pl.BlockSpec((tf, d), lambda i, j: (jnp.maximum(j - 1, 0), 0)),
                  pl.BlockSpec((tf, d), lambda i, j: (j, 0))],
        out_specs=pl.BlockSpec((tm, d), lambda i, j: (i, 0)),
        scratch_shapes=[pltpu.VMEM((tm, d), jnp.bfloat16),
                        pltpu.VMEM((nf, 2, SUBLANES, tf), jnp.float32),
                        pltpu.VMEM((2, SUBLANES + tm, FFN_CHUNK), jnp.float32),
                        pltpu.VMEM((2, SUBLANES + tm, FFN_CHUNK), jnp.float32)],
        compiler_params=_params(("arbitrary", "arbitrary"), 48),
        name="ffn",
    )(x1, g, sh, sc, g2, w_up, w_up, cw, cb, w_down, w_down)


ROPE_LO = 128


def _rope_tables(s):
    half = HEAD_DIM // 2
    inv = ROPE_THETA ** (-jnp.arange(half, dtype=jnp.float32) / half)
    lo = jnp.arange(ROPE_LO).astype(jnp.float32)[:, None] * inv[None, :]
    hi = (jnp.arange(s // ROPE_LO) * ROPE_LO).astype(jnp.float32)[:, None] * inv[None, :]
    cl, sl = jnp.cos(lo)[None], jnp.sin(lo)[None]
    ch, sh = jnp.cos(hi)[:, None], jnp.sin(hi)[:, None]
    cos = (ch * cl - sh * sl).reshape(s, half)
    sin = (sh * cl + ch * sl).reshape(s, half)
    return jnp.concatenate([cos, cos], axis=1), jnp.concatenate([-sin, sin], axis=1)


def kernel(x, c, w_ada, b_ada, norm_mix_g, w_in, q_norm_g, k_norm_g, w_pool_grp, pool_scale, w_attn_br,
           w_pool_br, w_gate, b_gate, w_o, norm_ffn_g, w_up, conv_w, conv_b, w_down):
    batch, s, d = x.shape
    assert batch == 1 and d == D_MODEL and w_ada.shape[0] == 1
    bf = jnp.bfloat16
    row = lambda v: v.reshape(1, -1)
    cos_t, sin_t = _rope_tables(s)

    mod = _ada(c.reshape(d, 1), w_ada[0], row(b_ada[0]))
    sh1, sc1, g1, sh2, sc2, g2 = [mod[:, n * d:(n + 1) * d] for n in range(6)]

    x2 = x[0]
    proj = _proj(x2, row(norm_mix_g[0]), sh1, sc1, w_in[0].astype(bf), w_gate[0].astype(bf), row(b_gate[0]),
                 cos_t, sin_t, row(q_norm_g[0]), row(k_norm_g[0]))
    bias = _select(proj)
    attn = _attention(proj, bias)
    pooled = _pool(proj, w_pool_grp[0].astype(bf), row(pool_scale[0]))
    x1 = _merge(attn, pooled, proj, x2, g1, w_attn_br[0].astype(bf), w_pool_br[0].astype(bf),
                w_o[0].astype(bf))
    out = _ffn(x1, row(norm_ffn_g[0]), sh2, sc2, g2, w_up[0].astype(bf), conv_w[0], conv_b[0],
               w_down[0].astype(bf))
    return out[None]
```

```python
import functools
import math

import jax
import jax.numpy as jnp
from jax import lax
from jax.experimental import pallas as pl
from jax.experimental.pallas import tpu as pltpu

D_MODEL = 2048
N_HEADS = 8
HEAD_DIM = 128
ATTN_W = N_HEADS * HEAD_DIM
MOBA_BLOCK = 256
MOBA_TOPK = 3
POOL_WINDOWS = (2, 4, 8, 16)
POOL_W = 1024
POOL_GW = POOL_W // len(POOL_WINDOWS)
D_FF = 5632
CONV_W = 3
ROPE_THETA = 10000.0
EPS = 1e-6
NEG = -1e30
BLOCK_SHIFT = MOBA_BLOCK.bit_length() - 1
LOG2E = math.log2(math.e)

LANES = 128
SUBLANES = 8
PROJ_W = 4 * ATTN_W + 2 * D_MODEL
Q_COL, K_COL, V_COL, U_COL, G_COL = 0, ATTN_W, 2 * ATTN_W, 3 * ATTN_W, 4 * ATTN_W

MIB = 1024 * 1024


def _dot(a, b):
    return jnp.dot(a, b, preferred_element_type=jnp.float32)


def _dot_nt(a, b):
    return lax.dot_general(a, b, (((1,), (1,)), ((), ())), preferred_element_type=jnp.float32)


def _sigmoid(x):
    return 0.5 * jnp.tanh(0.5 * x) + 0.5


def _params(semantics, vmem_mib):
    return pltpu.CompilerParams(dimension_semantics=semantics, vmem_limit_bytes=vmem_mib * MIB)


NORM_ROWS = 64


def _norm_modulate_into(h_scr, x_ref, g_ref, shift_ref, scale_ref):
    gain = g_ref[...] * (1.0 + scale_ref[...])
    shift = shift_ref[...]

    def body(r, carry):
        rows = pl.ds(pl.multiple_of(r * NORM_ROWS, NORM_ROWS), NORM_ROWS)
        x = x_ref[rows, :]
        inv = lax.rsqrt(jnp.mean(x * x, axis=-1, keepdims=True) + EPS)
        h_scr[rows, :] = (x * inv * gain + shift).astype(h_scr.dtype)
        return carry

    lax.fori_loop(0, x_ref.shape[0] // NORM_ROWS, body, 0, unroll=4)


def _ada_kernel(c_ref, w_ref, b_ref, o_ref):
    c = c_ref[...]
    s = c * jax.nn.sigmoid(c)
    o_ref[...] = jnp.sum(s * w_ref[...], axis=0, keepdims=True) + b_ref[...]


def _ada(c_col, w_ada, b_ada):
    d, n = w_ada.shape
    tn = 1024
    return pl.pallas_call(
        _ada_kernel,
        out_shape=jax.ShapeDtypeStruct((1, n), jnp.float32),
        grid=(n // tn,),
        in_specs=[pl.BlockSpec((d, 1), lambda j: (0, 0)),
                  pl.BlockSpec((d, tn), lambda j: (0, j)),
                  pl.BlockSpec((1, tn), lambda j: (0, j))],
        out_specs=pl.BlockSpec((1, tn), lambda j: (0, j)),
        compiler_params=_params(("arbitrary",), 40),
        name="ada",
    )(c_col, w_ada, b_ada)


def _proj_kernel(x_ref, g_ref, sh_ref, sc_ref, wi_ref, wg_ref, cb_ref, cos_ref, sin_ref, qg_ref, kg_ref,
                 o_ref, h_scr, acc_scr, *, tn):
    j = pl.program_id(1)

    @pl.when(j == 0)
    def _():
        _norm_modulate_into(h_scr, x_ref, g_ref, sh_ref, sc_ref)

    @pl.when(j < G_COL // tn)
    def _():
        acc_scr[...] = _dot(h_scr[...], wi_ref[...])

    @pl.when(j >= G_COL // tn)
    def _():
        acc_scr[...] = _dot(h_scr[...], wg_ref[...])

    def head_norm_rope(gain_ref, out_scale):
        for hh in range(tn // HEAD_DIM):
            t = acc_scr[:, hh * HEAD_DIM:(hh + 1) * HEAD_DIM]
            ms = jnp.mean(t * t, axis=-1, keepdims=True)
            y = t * lax.rsqrt(ms + EPS) * gain_ref[...]
            r = y * cos_ref[...] + pltpu.roll(y, HEAD_DIM // 2, axis=1) * sin_ref[...]
            if out_scale != 1.0:
                r = r * out_scale
            o_ref[:, hh * HEAD_DIM:(hh + 1) * HEAD_DIM] = r.astype(o_ref.dtype)

    @pl.when(j < K_COL // tn)
    def _():
        head_norm_rope(qg_ref, HEAD_DIM ** -0.5 * LOG2E)

    @pl.when(jnp.logical_and(j >= K_COL // tn, j < V_COL // tn))
    def _():
        head_norm_rope(kg_ref, 1.0)

    @pl.when(jnp.logical_and(j >= V_COL // tn, j < G_COL // tn))
    def _():
        o_ref[...] = acc_scr[...].astype(o_ref.dtype)

    @pl.when(j >= G_COL // tn)
    def _():
        o_ref[...] = _sigmoid(acc_scr[...] + cb_ref[...]).astype(o_ref.dtype)


def _proj(x2, g, sh, sc, w_in, w_gate, b_gate, cos_t, sin_t, qg, kg):
    s, d = x2.shape
    tm, tn = 1024, 512
    n_in = G_COL // tn
    vec = lambda: pl.BlockSpec((1, d), lambda i, j: (0, 0))
    return pl.pallas_call(
        functools.partial(_proj_kernel, tn=tn),
        out_shape=jax.ShapeDtypeStruct((s, PROJ_W), jnp.bfloat16),
        grid=(s // tm, PROJ_W // tn),
        in_specs=[pl.BlockSpec((tm, d), lambda i, j: (i, 0)),
                  vec(), vec(), vec(),
                  pl.BlockSpec((d, tn), lambda i, j: (0, jnp.minimum(j, n_in - 1))),
                  pl.BlockSpec((d, tn), lambda i, j: (0, jnp.maximum(j - n_in, 0))),
                  pl.BlockSpec((1, tn), lambda i, j: (0, jnp.maximum(j - n_in, 0))),
                  pl.BlockSpec((tm, HEAD_DIM), lambda i, j: (i, 0)),
                  pl.BlockSpec((tm, HEAD_DIM), lambda i, j: (i, 0)),
                  pl.BlockSpec((1, HEAD_DIM), lambda i, j: (0, 0)),
                  pl.BlockSpec((1, HEAD_DIM), lambda i, j: (0, 0))],
        out_specs=pl.BlockSpec((tm, tn), lambda i, j: (i, j)),
        scratch_shapes=[pltpu.VMEM((tm, d), jnp.bfloat16),
                        pltpu.VMEM((tm, tn), jnp.float32)],
        compiler_params=_params(("arbitrary", "arbitrary"), 48),
        name="proj",
    )(x2, g, sh, sc, w_in, w_gate, b_gate, cos_t, sin_t, qg, kg)


def _select_kernel(q_ref, k_ref, bias_ref, km_scr, *, tq):
    i = pl.program_id(1)
    nb = k_ref.shape[0] // MOBA_BLOCK

    @pl.when(i == 0)
    def _():
        k = k_ref[...].astype(jnp.float32).reshape(nb, MOBA_BLOCK, HEAD_DIM)
        km = jnp.sum(k, axis=1) * (1.0 / MOBA_BLOCK)
        p0 = km.astype(jnp.bfloat16)
        r1 = km - p0.astype(jnp.float32)
        p1 = r1.astype(jnp.bfloat16)
        p2 = (r1 - p1.astype(jnp.float32)).astype(jnp.bfloat16)
        km_scr[0] = p0
        km_scr[1] = p1
        km_scr[2] = p2

    q = q_ref[...]
    gate = _dot_nt(km_scr[0], q) + _dot_nt(km_scr[1], q) + _dot_nt(km_scr[2], q)
    blk = lax.broadcasted_iota(jnp.int32, gate.shape, 0)
    qpos = i * tq + lax.broadcasted_iota(jnp.int32, gate.shape, 1)
    own = lax.shift_right_logical(qpos, BLOCK_SHIFT)
    past = blk < own
    g = jnp.where(past, gate, NEG)
    bias = jnp.where(blk == own, 0.0, NEG)
    for _ in range(MOBA_TOPK):
        top = jnp.max(g, axis=0, keepdims=True)
        first = jnp.min(jnp.where(g == top, blk, nb), axis=0, keepdims=True)
        pick = blk == first
        bias = jnp.where(jnp.logical_and(pick, past), 0.0, bias)
        g = jnp.where(pick, -jnp.inf, g)
    bias_ref[...] = bias


def _eye(n):
    return (lax.broadcasted_iota(jnp.int32, (n, n), 0)
            == lax.broadcasted_iota(jnp.int32, (n, n), 1)).astype(jnp.bfloat16)


def _select(proj):
    s = proj.shape[0]
    tq = 4096
    kcol = K_COL // HEAD_DIM
    return pl.pallas_call(
        functools.partial(_select_kernel, tq=tq),
        out_shape=jax.ShapeDtypeStruct((N_HEADS, s // MOBA_BLOCK, s), jnp.float32),
        grid=(N_HEADS, s // tq),
        in_specs=[pl.BlockSpec((tq, HEAD_DIM), lambda h, i: (i, h)),
                  pl.BlockSpec((s, HEAD_DIM), lambda h, i: (0, kcol + h))],
        out_specs=pl.BlockSpec((None, s // MOBA_BLOCK, tq), lambda h, i: (h, 0, i)),
        scratch_shapes=[pltpu.VMEM((3, s // MOBA_BLOCK, HEAD_DIM), jnp.bfloat16)],
        compiler_params=_params(("arbitrary", "arbitrary"), 48),
        name="select",
    )(proj, proj)


ATTN_QB = 8
ATTN_UNROLL = 8
ONES_ROWS = 16
XPOSE_CHUNK = 2048


def _attn_kernel(q_ref, bias_ref, k_ref, v_ref, o_ref, vaug_scr, acc_scr, s_scr, p_scr, ptail_scr):
    i = pl.program_id(1)
    s_len = k_ref.shape[0]
    blk = MOBA_BLOCK

    @pl.when(i == 0)
    def _():
        for c in range(s_len // XPOSE_CHUNK):
            keys = slice(c * XPOSE_CHUNK, (c + 1) * XPOSE_CHUNK)
            vaug_scr[:HEAD_DIM, keys] = _dot_nt(_eye(HEAD_DIM), v_ref[keys, :]).astype(vaug_scr.dtype)
        vaug_scr[HEAD_DIM:, :] = jnp.ones((ONES_ROWS, s_len), vaug_scr.dtype)

    ng = ATTN_QB
    n_past = i * ng
    groups = [slice(g * blk, (g + 1) * blk) for g in range(ng)]
    key_le_query = (lax.broadcasted_iota(jnp.int32, (blk, blk), 0)
                    <= lax.broadcasted_iota(jnp.int32, (blk, blk), 1))

    def score_tile(j, slot, g, causal):
        kb = k_ref[pl.ds(pl.multiple_of(j * blk, blk), blk), :]
        s_t = _dot_nt(kb, q_ref[groups[g], :])
        if causal:
            s_t = jnp.where(key_le_query, s_t, NEG)
        s_scr[slot, :, groups[g]] = s_t
        return jnp.max(s_t, axis=0, keepdims=True)

    def softmax_tile(j, slot, g, m_old, mx):
        b = bias_ref[pl.ds(j, 1), groups[g]]
        m_new = jnp.maximum(m_old, mx + b)
        shift = jnp.maximum(m_new - b, mx)
        p_scr[slot, :, groups[g]] = jnp.exp2(s_scr[slot, :, groups[g]] - shift).astype(p_scr.dtype)
        return m_new, jnp.exp2(m_old - m_new)

    def value_tile(j, slot, g, alpha):
        vb = vaug_scr[:, pl.ds(pl.multiple_of(j * blk, blk), blk)]
        acc_scr[:, groups[g]] = alpha * acc_scr[:, groups[g]] + _dot(vb, p_scr[slot, :, groups[g]])

    def past_step(j, slot, m, mx, a_prev):
        mx_next, m_alpha = [], []
        for g in range(ng):
            value_tile(jnp.maximum(j - 1, 0), 1 - slot, g, a_prev[g])
            mx_next.append(score_tile(j + 1, 1 - slot, g, False))
            m_alpha.append(softmax_tile(j, slot, g, m[g], mx[g]))
        mx_next = tuple(mx_next)
        return tuple(ma[0] for ma in m_alpha), mx_next, tuple(ma[1] for ma in m_alpha)

    def body(jj, carry):
        for u in range(ATTN_UNROLL):
            carry = past_step(ATTN_UNROLL * jj + u, u % 2, *carry)
        return carry

    acc_scr[...] = jnp.zeros_like(acc_scr)
    p_scr[1] = jnp.zeros(p_scr.shape[1:], p_scr.dtype)
    init = (tuple(jnp.full((1, blk), NEG, jnp.float32) for _ in range(ng)),
            tuple(score_tile(0, 0, g, False) for g in range(ng)),
            tuple(jnp.ones((1, blk), jnp.float32) for _ in range(ng)))
    m, mx, alpha = lax.fori_loop(0, n_past // ATTN_UNROLL, body, init)
    m, mx, alpha = list(m), list(mx), list(alpha)

    for g in range(ng):
        value_tile(jnp.maximum(n_past - 1, 0), 1, g, alpha[g])
    mxs = {(g, 0): mx[g] for g in range(1, ng)}
    mxs[(0, 0)] = score_tile(n_past, 0, 0, True)
    for t in range(1, ng):
        for g in range(t, ng):
            mxs[(g, t)] = score_tile(n_past + t, t, g, g == t)
    span0 = pl.multiple_of(n_past * blk, blk)
    for g in range(ng):
        bs = [bias_ref[pl.ds(n_past + t, 1), groups[g]] for t in range(g + 1)]
        m_new = m[g]
        for t in range(g + 1):
            m_new = jnp.maximum(m_new, mxs[(g, t)] + bs[t])
        for t in range(g + 1):
            shift = jnp.maximum(m_new - bs[t], mxs[(g, t)])
            ptail_scr[t * blk:(t + 1) * blk, groups[g]] = (
                jnp.exp2(s_scr[t, :, groups[g]] - shift).astype(ptail_scr.dtype))
        pv = _dot(vaug_scr[:, pl.ds(span0, (g + 1) * blk)], ptail_scr[:(g + 1) * blk, groups[g]])
        acc_scr[:, groups[g]] = jnp.exp2(m[g] - m_new) * acc_scr[:, groups[g]] + pv
    o_ref[...] = (acc_scr[:HEAD_DIM, :] / acc_scr[HEAD_DIM:HEAD_DIM + 1, :]).T.astype(o_ref.dtype)


def _attention(proj, bias):
    s = proj.shape[0]
    tq = ATTN_QB * MOBA_BLOCK
    kcol, vcol = K_COL // HEAD_DIM, V_COL // HEAD_DIM
    return pl.pallas_call(
        _attn_kernel,
        out_shape=jax.ShapeDtypeStruct((s, ATTN_W), jnp.bfloat16),
        grid=(N_HEADS, s // tq),
        in_specs=[pl.BlockSpec((tq, HEAD_DIM), lambda h, i: (i, h)),
                  pl.BlockSpec((None, s // MOBA_BLOCK, tq), lambda h, i: (h, 0, i)),
                  pl.BlockSpec((s, HEAD_DIM), lambda h, i: (0, kcol + h)),
                  pl.BlockSpec((s, HEAD_DIM), lambda h, i: (0, vcol + h))],
        out_specs=pl.BlockSpec((tq, HEAD_DIM), lambda h, i: (i, h)),
        scratch_shapes=[pltpu.VMEM((HEAD_DIM + ONES_ROWS, s), jnp.bfloat16),
                        pltpu.VMEM((HEAD_DIM + ONES_ROWS, tq), jnp.float32),
                        pltpu.VMEM((ATTN_QB, MOBA_BLOCK, tq), jnp.float32),
                        pltpu.VMEM((2, MOBA_BLOCK, tq), jnp.bfloat16),
                        pltpu.VMEM((tq, tq), jnp.bfloat16)],
        compiler_params=_params(("arbitrary", "arbitrary"), 56),
        name="attention",
    )(proj, bias, proj, proj)


POOL_HALO = 16


def _pool_kernel(u_ref, halo_ref, w_ref, ls_ref, o_ref):
    i = pl.program_id(0)
    tm = u_ref.shape[0]
    u = u_ref[...].astype(jnp.float32)
    halo = jnp.where(i == 0, 0.0, halo_ref[...].astype(jnp.float32))
    t = i * tm + lax.broadcasted_iota(jnp.int32, (tm, 1), 0)
    for g, win in enumerate(POOL_WINDOWS):
        cols = slice(g * POOL_GW, (g + 1) * POOL_GW)
        ug = u[:, cols]
        e = jnp.concatenate([halo[:, cols], ug], axis=0)
        span = 1
        while span < win:
            e = e + pltpu.roll(e, span, axis=0)
            span *= 2
        cnt = jnp.minimum(t + 1, win).astype(jnp.float32)
        dg = e[POOL_HALO:, :] / cnt - ug
        mixed = _dot(dg.astype(jnp.bfloat16), w_ref[g])
        o_ref[:, cols] = (mixed * ls_ref[:, cols]).astype(o_ref.dtype)


def _pool(proj, w_grp, ls):
    s = proj.shape[0]
    tm = 1024
    ucol = U_COL // POOL_W
    return pl.pallas_call(
        _pool_kernel,
        out_shape=jax.ShapeDtypeStruct((s, POOL_W), jnp.bfloat16),
        grid=(s // tm,),
        in_specs=[pl.BlockSpec((tm, POOL_W), lambda i: (i, ucol)),
                  pl.BlockSpec((POOL_HALO, POOL_W),
                               lambda i: (jnp.maximum(i * (tm // POOL_HALO) - 1, 0), ucol)),
                  pl.BlockSpec(w_grp.shape, lambda i: (0, 0, 0)),
                  pl.BlockSpec((1, POOL_W), lambda i: (0, 0))],
        out_specs=pl.BlockSpec((tm, POOL_W), lambda i: (i, 0)),
        compiler_params=_params(("arbitrary",), 48),
        name="pool",
    )(proj, proj, w_grp, ls)


def _merge_kernel(a_ref, p_ref, ga_ref, gp_ref, x_ref, g1_ref, wa_ref, wp_ref, wo_ref, o_ref):
    j = pl.program_id(1)

    @pl.when(j == 0)
    def _():
        o_ref[...] = jnp.zeros_like(o_ref)

    ya = _dot(a_ref[...], wa_ref[...])
    yp = _dot(p_ref[...], wp_ref[...])
    merged = ga_ref[...].astype(jnp.float32) * ya + gp_ref[...].astype(jnp.float32) * yp
    o_ref[...] += _dot(merged.astype(jnp.bfloat16), wo_ref[...])

    @pl.when(j == pl.num_programs(1) - 1)
    def _():
        o_ref[...] = x_ref[...] + g1_ref[...] * o_ref[...]


def _merge(attn, pooled, proj, x2, g1, wa, wp, wo):
    s, d = x2.shape
    tm, tn = 512, 512
    ga0, gp0 = G_COL // tn, (G_COL + d) // tn
    return pl.pallas_call(
        _merge_kernel,
        out_shape=jax.ShapeDtypeStruct((s, d), jnp.float32),
        grid=(s // tm, d // tn),
        in_specs=[pl.BlockSpec((tm, ATTN_W), lambda i, j: (i, 0)),
                  pl.BlockSpec((tm, POOL_W), lambda i, j: (i, 0)),
                  pl.BlockSpec((tm, tn), lambda i, j: (i, ga0 + j)),
                  pl.BlockSpec((tm, tn), lambda i, j: (i, gp0 + j)),
                  pl.BlockSpec((tm, d), lambda i, j: (i, 0)),
                  pl.BlockSpec((1, d), lambda i, j: (0, 0)),
                  pl.BlockSpec((ATTN_W, tn), lambda i, j: (0, j)),
                  pl.BlockSpec((POOL_W, tn), lambda i, j: (0, j)),
                  pl.BlockSpec((tn, d), lambda i, j: (j, 0))],
        out_specs=pl.BlockSpec((tm, d), lambda i, j: (i, 0)),
        compiler_params=_params(("arbitrary", "arbitrary"), 48),
        name="merge",
    )(attn, pooled, proj, proj, x2, g1, wa, wp, wo)


FFN_CHUNK = 256


def _ffn_kernel(x_ref, g_ref, sh_ref, sc_ref, g2_ref, wua_ref, wub_ref, cw_ref, cb_ref, wdp_ref, wd_ref,
                o_ref, h_scr, tail_scr, upa_scr, upb_scr):
    i = pl.program_id(0)
    j = pl.program_id(1)
    tm = x_ref.shape[0]
    n_chunks = cw_ref.shape[0] // 2

    @pl.when(j == 0)
    def _():
        _norm_modulate_into(h_scr, x_ref, g_ref, sh_ref, sc_ref)
        o_ref[...] = jnp.zeros_like(o_ref)

    @pl.when(i == 0)
    def _():
        tail_scr[j] = jnp.zeros(tail_scr.shape[1:], tail_scr.dtype)

    chunks = [slice(c * FFN_CHUNK, (c + 1) * FFN_CHUNK) for c in range(2)]

    def up_stage(c):
        for half, (w_ref, u_scr) in enumerate(((wua_ref, upa_scr), (wub_ref, upb_scr))):
            u_scr[c, :SUBLANES, :] = tail_scr[j, half, :, chunks[c]]
            up = _dot(h_scr[...], w_ref[:, chunks[c]])
            u_scr[c, SUBLANES:, :] = up
            tail_scr[j, half, :, chunks[c]] = up[tm - SUBLANES:, :]

    def conv(ext, n):
        cw = cw_ref[n]
        out = cb_ref[n] + cw[CONV_W - 1:CONV_W, :] * ext[SUBLANES:, :]
        for back in range(1, CONV_W):
            tap = CONV_W - 1 - back
            out = out + cw[tap:tap + 1, :] * pltpu.roll(ext, back, axis=0)[SUBLANES:, :]
        return out

    def gated(c, n):
        a = conv(upa_scr[c], n)
        b = conv(upb_scr[c], n_chunks + n)
        return (a * _sigmoid(a) * b).astype(jnp.bfloat16)

    up_stage(0)
    act_prev = gated(1, jnp.maximum(2 * j - 1, 0))
    act_prev = jnp.where(j > 0, act_prev, jnp.zeros_like(act_prev))
    up_stage(1)
    act_0 = gated(0, 2 * j)
    o_ref[...] += _dot(act_prev, wdp_ref[chunks[1], :]) + _dot(act_0, wd_ref[chunks[0], :])

    @pl.when(j == pl.num_programs(1) - 1)
    def _():
        o_ref[...] += _dot(gated(1, 2 * j + 1), wd_ref[chunks[1], :])
        o_ref[...] = x_ref[...] + g2_ref[...] * o_ref[...]


def _ffn(x1, g, sh, sc, g2, w_up, conv_w, conv_b, w_down):
    s, d = x1.shape
    tm, tf = 512, 2 * FFN_CHUNK
    nf = D_FF // tf
    n_chunks = 2 * D_FF // FFN_CHUNK
    cw = conv_w.reshape(CONV_W, n_chunks, FFN_CHUNK).transpose(1, 0, 2)
    cb = conv_b.reshape(n_chunks, 1, FFN_CHUNK)
    vec = lambda: pl.BlockSpec((1, d), lambda i, j: (0, 0))
    return pl.pallas_call(
        _ffn_kernel,
        out_shape=jax.ShapeDtypeStruct((s, d), jnp.float32),
        grid=(s // tm, nf),
        in_specs=[pl.BlockSpec((tm, d), lambda i, j: (i, 0)),
                  vec(), vec(), vec(), vec(),
                  pl.BlockSpec((d, tf), lambda i, j: (0, j)),
                  pl.BlockSpec((d, tf), lambda i, j: (0, nf + j)),
                  pl.BlockSpec(cw.shape, lambda i, j: (0, 0, 0)),
                  pl.BlockSpec(cb.shape, lambda i, j: (0, 0, 0)),
                  pl.BlockSpec((tf, d), lambda i, j: (jnp.maximum(j - 1, 0), 0)),
                  pl.BlockSpec((tf, d), lambda i, j: (j, 0))],
        out_specs=pl.BlockSpec((tm, d), lambda i, j: (i, 0)),
        scratch_shapes=[pltpu.VMEM((tm, d), jnp.bfloat16),
                        pltpu.VMEM((nf, 2, SUBLANES, tf), jnp.float32),
                        pltpu.VMEM((2, SUBLANES + tm, FFN_CHUNK), jnp.float32),
                        pltpu.VMEM((2, SUBLANES + tm, FFN_CHUNK), jnp.float32)],
        compiler_params=_params(("arbitrary", "arbitrary"), 48),
        name="ffn",
    )(x1, g, sh, sc, g2, w_up, w_up, cw, cb, w_down, w_down)


ROPE_LO = 128


def _rope_tables(s):
    half = HEAD_DIM // 2
    inv = ROPE_THETA ** (-jnp.arange(half, dtype=jnp.float32) / half)
    lo = jnp.arange(ROPE_LO).astype(jnp.float32)[:, None] * inv[None, :]
    hi = (jnp.arange(s // ROPE_LO) * ROPE_LO).astype(jnp.float32)[:, None] * inv[None, :]
    cl, sl = jnp.cos(lo)[None], jnp.sin(lo)[None]
    ch, sh = jnp.cos(hi)[:, None], jnp.sin(hi)[:, None]
    cos = (ch * cl - sh * sl).reshape(s, half)
    sin = (sh * cl + ch * sl).reshape(s, half)
    return jnp.concatenate([cos, cos], axis=1), jnp.concatenate([-sin, sin], axis=1)


def kernel(x, c, w_ada, b_ada, norm_mix_g, w_in, q_norm_g, k_norm_g, w_pool_grp, pool_scale, w_attn_br,
           w_pool_br, w_gate, b_gate, w_o, norm_ffn_g, w_up, conv_w, conv_b, w_down):
    batch, s, d = x.shape
    assert batch == 1 and d == D_MODEL and w_ada.shape[0] == 1
    bf = jnp.bfloat16
    row = lambda v: v.reshape(1, -1)
    cos_t, sin_t = _rope_tables(s)

    mod = _ada(c.reshape(d, 1), w_ada[0], row(b_ada[0]))
    sh1, sc1, g1, sh2, sc2, g2 = [mod[:, n * d:(n + 1) * d] for n in range(6)]

    x2 = x[0]
    proj = _proj(x2, row(norm_mix_g[0]), sh1, sc1, w_in[0].astype(bf), w_gate[0].astype(bf), row(b_gate[0]),
                 cos_t, sin_t, row(q_norm_g[0]), row(k_norm_g[0]))
    bias = _select(proj)
    attn = _attention(proj, bias)
    pooled = _pool(proj, w_pool_grp[0].astype(bf), row(pool_scale[0]))
    x1 = _merge(attn, pooled, proj, x2, g1, w_attn_br[0].astype(bf), w_pool_br[0].astype(bf),
                w_o[0].astype(bf))
    out = _ffn(x1, row(norm_ffn_g[0]), sh2, sc2, g2, w_up[0].astype(bf), conv_w[0], conv_b[0],
               w_down[0].astype(bf))
    return out[None]
```

```python
import functools
import math

import jax
import jax.numpy as jnp
from jax import lax
from jax.experimental import pallas as pl
from jax.experimental.pallas import tpu as pltpu

D_MODEL = 2048
N_HEADS = 8
HEAD_DIM = 128
ATTN_W = N_HEADS * HEAD_DIM
MOBA_BLOCK = 256
MOBA_TOPK = 3
POOL_WINDOWS = (2, 4, 8, 16)
POOL_W = 1024
POOL_GW = POOL_W // len(POOL_WINDOWS)
D_FF = 5632
CONV_W = 3
ROPE_THETA = 10000.0
EPS = 1e-6
NEG = -1e30
BLOCK_SHIFT = MOBA_BLOCK.bit_length() - 1
LOG2E = math.log2(math.e)

LANES = 128
SUBLANES = 8
PROJ_W = 4 * ATTN_W + 2 * D_MODEL
Q_COL, K_COL, V_COL, U_COL, G_COL = 0, ATTN_W, 2 * ATTN_W, 3 * ATTN_W, 4 * ATTN_W

MIB = 1024 * 1024


def _dot(a, b):
    return jnp.dot(a, b, preferred_element_type=jnp.float32)


def _dot_nt(a, b):
    return lax.dot_general(a, b, (((1,), (1,)), ((), ())), preferred_element_type=jnp.float32)


def _sigmoid(x):
    return 0.5 * jnp.tanh(0.5 * x) + 0.5


def _params(semantics, vmem_mib):
    return pltpu.CompilerParams(dimension_semantics=semantics, vmem_limit_bytes=vmem_mib * MIB)


NORM_ROWS = 64


def _norm_modulate_into(h_scr, x_ref, g_ref, shift_ref, scale_ref):
    gain = g_ref[...] * (1.0 + scale_ref[...])
    shift = shift_ref[...]

    def body(r, carry):
        rows = pl.ds(pl.multiple_of(r * NORM_ROWS, NORM_ROWS), NORM_ROWS)
        x = x_ref[rows, :]
        inv = lax.rsqrt(jnp.mean(x * x, axis=-1, keepdims=True) + EPS)
        h_scr[rows, :] = (x * inv * gain + shift).astype(h_scr.dtype)
        return carry

    lax.fori_loop(0, x_ref.shape[0] // NORM_ROWS, body, 0, unroll=4)


def _ada_kernel(c_ref, w_ref, b_ref, o_ref):
    c = c_ref[...]
    s = c * jax.nn.sigmoid(c)
    o_ref[...] = jnp.sum(s * w_ref[...], axis=0, keepdims=True) + b_ref[...]


def _ada(c_col, w_ada, b_ada):
    d, n = w_ada.shape
    tn = 1024
    return pl.pallas_call(
        _ada_kernel,
        out_shape=jax.ShapeDtypeStruct((1, n), jnp.float32),
        grid=(n // tn,),
        in_specs=[pl.BlockSpec((d, 1), lambda j: (0, 0)),
                  pl.BlockSpec((d, tn), lambda j: (0, j)),
                  pl.BlockSpec((1, tn), lambda j: (0, j))],
        out_specs=pl.BlockSpec((1, tn), lambda j: (0, j)),
        compiler_params=_params(("arbitrary",), 40),
        name="ada",
    )(c_col, w_ada, b_ada)


def _proj_kernel(x_ref, g_ref, sh_ref, sc_ref, wi_ref, wg_ref, cb_ref, cos_ref, sin_ref, qg_ref, kg_ref,
                 o_ref, h_scr, acc_scr, *, tn):
    j = pl.program_id(1)

    @pl.when(j == 0)
    def _():
        _norm_modulate_into(h_scr, x_ref, g_ref, sh_ref, sc_ref)

    @pl.when(j < G_COL // tn)
    def _():
        acc_scr[...] = _dot(h_scr[...], wi_ref[...])

    @pl.when(j >= G_COL // tn)
    def _():
        acc_scr[...] = _dot(h_scr[...], wg_ref[...])

    def head_norm_rope(gain_ref, out_scale):
        for hh in range(tn // HEAD_DIM):
            t = acc_scr[:, hh * HEAD_DIM:(hh + 1) * HEAD_DIM]
            ms = jnp.mean(t * t, axis=-1, keepdims=True)
            y = t * lax.rsqrt(ms + EPS) * gain_ref[...]
            r = y * cos_ref[...] + pltpu.roll(y, HEAD_DIM // 2, axis=1) * sin_ref[...]
            if out_scale != 1.0:
                r = r * out_scale
            o_ref[:, hh * HEAD_DIM:(hh + 1) * HEAD_DIM] = r.astype(o_ref.dtype)

    @pl.when(j < K_COL // tn)
    def _():
        head_norm_rope(qg_ref, HEAD_DIM ** -0.5 * LOG2E)

    @pl.when(jnp.logical_and(j >= K_COL // tn, j < V_COL // tn))
    def _():
        head_norm_rope(kg_ref, 1.0)

    @pl.when(jnp.logical_and(j >= V_COL // tn, j < G_COL // tn))
    def _():
        o_ref[...] = acc_scr[...].astype(o_ref.dtype)

    @pl.when(j >= G_COL // tn)
    def _():
        o_ref[...] = _sigmoid(acc_scr[...] + cb_ref[...]).astype(o_ref.dtype)


def _proj(x2, g, sh, sc, w_in, w_gate, b_gate, cos_t, sin_t, qg, kg):
    s, d = x2.shape
    tm, tn = 1024, 1024
    n_in = G_COL // tn
    vec = lambda: pl.BlockSpec((1, d), lambda i, j: (0, 0))
    return pl.pallas_call(
        functools.partial(_proj_kernel, tn=tn),
        out_shape=jax.ShapeDtypeStruct((s, PROJ_W), jnp.bfloat16),
        grid=(s // tm, PROJ_W // tn),
        in_specs=[pl.BlockSpec((tm, d), lambda i, j: (i, 0)),
                  vec(), vec(), vec(),
                  pl.BlockSpec((d, tn), lambda i, j: (0, jnp.minimum(j, n_in - 1))),
                  pl.BlockSpec((d, tn), lambda i, j: (0, jnp.maximum(j - n_in, 0))),
                  pl.BlockSpec((1, tn), lambda i, j: (0, jnp.maximum(j - n_in, 0))),
                  pl.BlockSpec((tm, HEAD_DIM), lambda i, j: (i, 0)),
                  pl.BlockSpec((tm, HEAD_DIM), lambda i, j: (i, 0)),
                  pl.BlockSpec((1, HEAD_DIM), lambda i, j: (0, 0)),
                  pl.BlockSpec((1, HEAD_DIM), lambda i, j: (0, 0))],
        out_specs=pl.BlockSpec((tm, tn), lambda i, j: (i, j)),
        scratch_shapes=[pltpu.VMEM((tm, d), jnp.bfloat16),
                        pltpu.VMEM((tm, tn), jnp.float32)],
        compiler_params=_params(("arbitrary", "arbitrary"), 56),
        name="proj",
    )(x2, g, sh, sc, w_in, w_gate, b_gate, cos_t, sin_t, qg, kg)


def _select_kernel(q_ref, k_ref, bias_ref, km_scr, *, tq):
    i = pl.program_id(1)
    nb = k_ref.shape[0] // MOBA_BLOCK

    @pl.when(i == 0)
    def _():
        k = k_ref[...].astype(jnp.float32).reshape(nb, MOBA_BLOCK, HEAD_DIM)
        km = jnp.sum(k, axis=1) * (1.0 / MOBA_BLOCK)
        p0 = km.astype(jnp.bfloat16)
        r1 = km - p0.astype(jnp.float32)
        p1 = r1.astype(jnp.bfloat16)
        p2 = (r1 - p1.astype(jnp.float32)).astype(jnp.bfloat16)
        km_scr[0] = p0
        km_scr[1] = p1
        km_scr[2] = p2

    q = q_ref[...]
    gate = _dot_nt(km_scr[0], q) + _dot_nt(km_scr[1], q) + _dot_nt(km_scr[2], q)
    blk = lax.broadcasted_iota(jnp.int32, gate.shape, 0)
    qpos = i * tq + lax.broadcasted_iota(jnp.int32, gate.shape, 1)
    own = lax.shift_right_logical(qpos, BLOCK_SHIFT)
    past = blk < own
    g = jnp.where(past, gate, NEG)
    bias = jnp.where(blk == own, 0.0, NEG)
    for _ in range(MOBA_TOPK):
        top = jnp.max(g, axis=0, keepdims=True)
        first = jnp.min(jnp.where(g == top, blk, nb), axis=0, keepdims=True)
        pick = blk == first
        bias = jnp.where(jnp.logical_and(pick, past), 0.0, bias)
        g = jnp.where(pick, -jnp.inf, g)
    bias_ref[...] = bias


def _eye(n):
    return (lax.broadcasted_iota(jnp.int32, (n, n), 0)
            == lax.broadcasted_iota(jnp.int32, (n, n), 1)).astype(jnp.bfloat16)


def _select(proj):
    s = proj.shape[0]
    tq = 4096
    kcol = K_COL // HEAD_DIM
    return pl.pallas_call(
        functools.partial(_select_kernel, tq=tq),
        out_shape=jax.ShapeDtypeStruct((N_HEADS, s // MOBA_BLOCK, s), jnp.float32),
        grid=(N_HEADS, s // tq),
        in_specs=[pl.BlockSpec((tq, HEAD_DIM), lambda h, i: (i, h)),
                  pl.BlockSpec((s, HEAD_DIM), lambda h, i: (0, kcol + h))],
        out_specs=pl.BlockSpec((None, s // MOBA_BLOCK, tq), lambda h, i: (h, 0, i)),
        scratch_shapes=[pltpu.VMEM((3, s // MOBA_BLOCK, HEAD_DIM), jnp.bfloat16)],
        compiler_params=_params(("arbitrary", "arbitrary"), 48),
        name="select",
    )(proj, proj)


ATTN_QB = 8
ATTN_UNROLL = 8
ONES_ROWS = 16
XPOSE_CHUNK = 2048


def _attn_kernel(q_ref, bias_ref, k_ref, v_ref, o_ref, vaug_scr, acc_scr, s_scr, p_scr, ptail_scr):
    i = pl.program_id(1)
    s_len = k_ref.shape[0]
    blk = MOBA_BLOCK

    @pl.when(i == 0)
    def _():
        for c in range(s_len // XPOSE_CHUNK):
            keys = slice(c * XPOSE_CHUNK, (c + 1) * XPOSE_CHUNK)
            vaug_scr[:HEAD_DIM, keys] = _dot_nt(_eye(HEAD_DIM), v_ref[keys, :]).astype(vaug_scr.dtype)
        vaug_scr[HEAD_DIM:, :] = jnp.ones((ONES_ROWS, s_len), vaug_scr.dtype)

    ng = ATTN_QB
    n_past = i * ng
    groups = [slice(g * blk, (g + 1) * blk) for g in range(ng)]
    key_le_query = (lax.broadcasted_iota(jnp.int32, (blk, blk), 0)
                    <= lax.broadcasted_iota(jnp.int32, (blk, blk), 1))

    def score_tile(j, slot, g, causal):
        kb = k_ref[pl.ds(pl.multiple_of(j * blk, blk), blk), :]
        s_t = _dot_nt(kb, q_ref[groups[g], :])
        if causal:
            s_t = jnp.where(key_le_query, s_t, NEG)
        s_scr[slot, :, groups[g]] = s_t
        return jnp.max(s_t, axis=0, keepdims=True)

    def softmax_tile(j, slot, g, m_old, mx):
        b = bias_ref[pl.ds(j, 1), groups[g]]
        m_new = jnp.maximum(m_old, mx + b)
        shift = jnp.maximum(m_new - b, mx)
        p_scr[slot, :, groups[g]] = jnp.exp2(s_scr[slot, :, groups[g]] - shift).astype(p_scr.dtype)
        return m_new, jnp.exp2(m_old - m_new)

    def value_tile(j, slot, g, alpha):
        vb = vaug_scr[:, pl.ds(pl.multiple_of(j * blk, blk), blk)]
        acc_scr[:, groups[g]] = alpha * acc_scr[:, groups[g]] + _dot(vb, p_scr[slot, :, groups[g]])

    def past_step(j, slot, m, mx, a_prev):
        mx_next, m_alpha = [], []
        for g in range(ng):
            value_tile(jnp.maximum(j - 1, 0), 1 - slot, g, a_prev[g])
            mx_next.append(score_tile(j + 1, 1 - slot, g, False))
            m_alpha.append(softmax_tile(j, slot, g, m[g], mx[g]))
        mx_next = tuple(mx_next)
        return tuple(ma[0] for ma in m_alpha), mx_next, tuple(ma[1] for ma in m_alpha)

    def body(jj, carry):
        for u in range(ATTN_UNROLL):
            carry = past_step(ATTN_UNROLL * jj + u, u % 2, *carry)
        return carry

    acc_scr[...] = jnp.zeros_like(acc_scr)
    p_scr[1] = jnp.zeros(p_scr.shape[1:], p_scr.dtype)
    init = (tuple(jnp.full((1, blk), NEG, jnp.float32) for _ in range(ng)),
            tuple(score_tile(0, 0, g, False) for g in range(ng)),
            tuple(jnp.ones((1, blk), jnp.float32) for _ in range(ng)))
    m, mx, alpha = lax.fori_loop(0, n_past // ATTN_UNROLL, body, init)
    m, mx, alpha = list(m), list(mx), list(alpha)

    for g in range(ng):
        value_tile(jnp.maximum(n_past - 1, 0), 1, g, alpha[g])
    mxs = {(g, 0): mx[g] for g in range(1, ng)}
    mxs[(0, 0)] = score_tile(n_past, 0, 0, True)
    for t in range(1, ng):
        for g in range(t, ng):
            mxs[(g, t)] = score_tile(n_past + t, t, g, g == t)
    span0 = pl.multiple_of(n_past * blk, blk)
    for g in range(ng):
        bs = [bias_ref[pl.ds(n_past + t, 1), groups[g]] for t in range(g + 1)]
        m_new = m[g]
        for t in range(g + 1):
            m_new = jnp.maximum(m_new, mxs[(g, t)] + bs[t])
        for t in range(g + 1):
            shift = jnp.maximum(m_new - bs[t], mxs[(g, t)])
            ptail_scr[t * blk:(t + 1) * blk, groups[g]] = (
                jnp.exp2(s_scr[t, :, groups[g]] - shift).astype(ptail_scr.dtype))
        pv = _dot(vaug_scr[:, pl.ds(span0, (g + 1) * blk)], ptail_scr[:(g + 1) * blk, groups[g]])
        acc_scr[:, groups[g]] = jnp.exp2(m[g] - m_new) * acc_scr[:, groups[g]] + pv
    o_ref[...] = (acc_scr[:HEAD_DIM, :] / acc_scr[HEAD_DIM:HEAD_DIM + 1, :]).T.astype(o_ref.dtype)


def _attention(proj, bias):
    s = proj.shape[0]
    tq = ATTN_QB * MOBA_BLOCK
    kcol, vcol = K_COL // HEAD_DIM, V_COL // HEAD_DIM
    return pl.pallas_call(
        _attn_kernel,
        out_shape=jax.ShapeDtypeStruct((s, ATTN_W), jnp.bfloat16),
        grid=(N_HEADS, s // tq),
        in_specs=[pl.BlockSpec((tq, HEAD_DIM), lambda h, i: (i, h)),
                  pl.BlockSpec((None, s // MOBA_BLOCK, tq), lambda h, i: (h, 0, i)),
                  pl.BlockSpec((s, HEAD_DIM), lambda h, i: (0, kcol + h)),
                  pl.BlockSpec((s, HEAD_DIM), lambda h, i: (0, vcol + h))],
        out_specs=pl.BlockSpec((tq, HEAD_DIM), lambda h, i: (i, h)),
        scratch_shapes=[pltpu.VMEM((HEAD_DIM + ONES_ROWS, s), jnp.bfloat16),
                        pltpu.VMEM((HEAD_DIM + ONES_ROWS, tq), jnp.float32),
                        pltpu.VMEM((ATTN_QB, MOBA_BLOCK, tq), jnp.float32),
                        pltpu.VMEM((2, MOBA_BLOCK, tq), jnp.bfloat16),
                        pltpu.VMEM((tq, tq), jnp.bfloat16)],
        compiler_params=_params(("arbitrary", "arbitrary"), 56),
        name="attention",
    )(proj, bias, proj, proj)


POOL_HALO = 16


def _pool_kernel(u_ref, halo_ref, w_ref, ls_ref, o_ref):
    i = pl.program_id(0)
    tm = u_ref.shape[0]
    u = u_ref[...].astype(jnp.float32)
    halo = jnp.where(i == 0, 0.0, halo_ref[...].astype(jnp.float32))
    t = i * tm + lax.broadcasted_iota(jnp.int32, (tm, 1), 0)
    for g, win in enumerate(POOL_WINDOWS):
        cols = slice(g * POOL_GW, (g + 1) * POOL_GW)
        ug = u[:, cols]
        e = jnp.concatenate([halo[:, cols], ug], axis=0)
        span = 1
        while span < win:
            e = e + pltpu.roll(e, span, axis=0)
            span *= 2
        cnt = jnp.minimum(t + 1, win).astype(jnp.float32)
        dg = e[POOL_HALO:, :] / cnt - ug
        mixed = _dot(dg.astype(jnp.bfloat16), w_ref[g])
        o_ref[:, cols] = (mixed * ls_ref[:, cols]).astype(o_ref.dtype)


def _pool(proj, w_grp, ls):
    s = proj.shape[0]
    tm = 1024
    ucol = U_COL // POOL_W
    return pl.pallas_call(
        _pool_kernel,
        out_shape=jax.ShapeDtypeStruct((s, POOL_W), jnp.bfloat16),
        grid=(s // tm,),
        in_specs=[pl.BlockSpec((tm, POOL_W), lambda i: (i, ucol)),
                  pl.BlockSpec((POOL_HALO, POOL_W),
                               lambda i: (jnp.maximum(i * (tm // POOL_HALO) - 1, 0), ucol)),
                  pl.BlockSpec(w_grp.shape, lambda i: (0, 0, 0)),
                  pl.BlockSpec((1, POOL_W), lambda i: (0, 0))],
        out_specs=pl.BlockSpec((tm, POOL_W), lambda i: (i, 0)),
        compiler_params=_params(("arbitrary",), 48),
        name="pool",
    )(proj, proj, w_grp, ls)


def _merge_kernel(a_ref, p_ref, ga_ref, gp_ref, x_ref, g1_ref, wa_ref, wp_ref, wo_ref, o_ref):
    j = pl.program_id(1)

    @pl.when(j == 0)
    def _():
        o_ref[...] = jnp.zeros_like(o_ref)

    ya = _dot(a_ref[...], wa_ref[...])
    yp = _dot(p_ref[...], wp_ref[...])
    merged = ga_ref[...].astype(jnp.float32) * ya + gp_ref[...].astype(jnp.float32) * yp
    o_ref[...] += _dot(merged.astype(jnp.bfloat16), wo_ref[...])

    @pl.when(j == pl.num_programs(1) - 1)
    def _():
        o_ref[...] = x_ref[...] + g1_ref[...] * o_ref[...]


def _merge(attn, pooled, proj, x2, g1, wa, wp, wo):
    s, d = x2.shape
    tm, tn = 512, 1024
    ga0, gp0 = G_COL // tn, (G_COL + d) // tn
    return pl.pallas_call(
        _merge_kernel,
        out_shape=jax.ShapeDtypeStruct((s, d), jnp.float32),
        grid=(s // tm, d // tn),
        in_specs=[pl.BlockSpec((tm, ATTN_W), lambda i, j: (i, 0)),
                  pl.BlockSpec((tm, POOL_W), lambda i, j: (i, 0)),
                  pl.BlockSpec((tm, tn), lambda i, j: (i, ga0 + j)),
                  pl.BlockSpec((tm, tn), lambda i, j: (i, gp0 + j)),
                  pl.BlockSpec((tm, d), lambda i, j: (i, 0)),
                  pl.BlockSpec((1, d), lambda i, j: (0, 0)),
                  pl.BlockSpec((ATTN_W, tn), lambda i, j: (0, j)),
                  pl.BlockSpec((POOL_W, tn), lambda i, j: (0, j)),
                  pl.BlockSpec((tn, d), lambda i, j: (j, 0))],
        out_specs=pl.BlockSpec((tm, d), lambda i, j: (i, 0)),
        compiler_params=_params(("arbitrary", "arbitrary"), 56),
        name="merge",
    )(attn, pooled, proj, proj, x2, g1, wa, wp, wo)


FFN_CHUNK = 256


def _ffn_kernel(x_ref, g_ref, sh_ref, sc_ref, g2_ref, wua_ref, wub_ref, cw_ref, cb_ref, wdp_ref, wd_ref,
                o_ref, h_scr, tail_scr, upa_scr, upb_scr):
    i = pl.program_id(0)
    j = pl.program_id(1)
    tm = x_ref.shape[0]
    n_chunks = cw_ref.shape[0] // 2

    @pl.when(j == 0)
    def _():
        _norm_modulate_into(h_scr, x_ref, g_ref, sh_ref, sc_ref)
        o_ref[...] = jnp.zeros_like(o_ref)

    @pl.when(i == 0)
    def _():
        tail_scr[j] = jnp.zeros(tail_scr.shape[1:], tail_scr.dtype)

    chunks = [slice(c * FFN_CHUNK, (c + 1) * FFN_CHUNK) for c in range(2)]

    def up_stage(c):
        for half, (w_ref, u_scr) in enumerate(((wua_ref, upa_scr), (wub_ref, upb_scr))):
            u_scr[c, :SUBLANES, :] = tail_scr[j, half, :, chunks[c]]
            up = _dot(h_scr[...], w_ref[:, chunks[c]])
            u_scr[c, SUBLANES:, :] = up
            tail_scr[j, half, :, chunks[c]] = up[tm - SUBLANES:, :]

    def conv(ext, n):
        cw = cw_ref[n]
        out = cb_ref[n] + cw[CONV_W - 1:CONV_W, :] * ext[SUBLANES:, :]
        for back in range(1, CONV_W):
            tap = CONV_W - 1 - back
            out = out + cw[tap:tap + 1, :] * pltpu.roll(ext, back, axis=0)[SUBLANES:, :]
        return out

    def gated(c, n):
        a = conv(upa_scr[c], n)
        b = conv(upb_scr[c], n_chunks + n)
        return (a * _sigmoid(a) * b).astype(jnp.bfloat16)

    up_stage(0)
    act_prev = gated(1, jnp.maximum(2 * j - 1, 0))
    act_prev = jnp.where(j > 0, act_prev, jnp.zeros_like(act_prev))
    up_stage(1)
    act_0 = gated(0, 2 * j)
    o_ref[...] += _dot(act_prev, wdp_ref[chunks[1], :]) + _dot(act_0, wd_ref[chunks[0], :])

    @pl.when(j == pl.num_programs(1) - 1)
    def _():
        o_ref[...] += _dot(gated(1, 2 * j + 1), wd_ref[chunks[1], :])
        o_ref[...] = x_ref[...] + g2_ref[...] * o_ref[...]


def _ffn(x1, g, sh, sc, g2, w_up, conv_w, conv_b, w_down):
    s, d = x1.shape
    tm, tf = 512, 2 * FFN_CHUNK
    nf = D_FF // tf
    n_chunks = 2 * D_FF // FFN_CHUNK
    cw = conv_w.reshape(CONV_W, n_chunks, FFN_CHUNK).transpose(1, 0, 2)
    cb = conv_b.reshape(n_chunks, 1, FFN_CHUNK)
    vec = lambda: pl.BlockSpec((1, d), lambda i, j: (0, 0))
    return pl.pallas_call(
        _ffn_kernel,
        out_shape=jax.ShapeDtypeStruct((s, d), jnp.float32),
        grid=(s // tm, nf),
        in_specs=[pl.BlockSpec((tm, d), lambda i, j: (i, 0)),
                  vec(), vec(), vec(), vec(),
                  pl.BlockSpec((d, tf), lambda i, j: (0, j)),
                  pl.BlockSpec((d, tf), lambda i, j: (0, nf + j)),
                  pl.BlockSpec(cw.shape, lambda i, j: (0, 0, 0)),
                  pl.BlockSpec(cb.shape, lambda i, j: (0, 0, 0)),
                  pl.BlockSpec((tf, d), lambda i, j: (jnp.maximum(j - 1, 0), 0)),
                  pl.BlockSpec((tf, d), lambda i, j: (j, 0))],
        out_specs=pl.BlockSpec((tm, d), lambda i, j: (i, 0)),
        scratch_shapes=[pltpu.VMEM((tm, d), jnp.bfloat16),
                        pltpu.VMEM((nf, 2, SUBLANES, tf), jnp.float32),
                        pltpu.VMEM((2, SUBLANES + tm, FFN_CHUNK), jnp.float32),
                        pltpu.VMEM((2, SUBLANES + tm, FFN_CHUNK), jnp.float32)],
        compiler_params=_params(("arbitrary", "arbitrary"), 48),
        name="ffn",
    )(x1, g, sh, sc, g2, w_up, w_up, cw, cb, w_down, w_down)


ROPE_LO = 128


def _rope_tables(s):
    half = HEAD_DIM // 2
    inv = ROPE_THETA ** (-jnp.arange(half, dtype=jnp.float32) / half)
    lo = jnp.arange(ROPE_LO).astype(jnp.float32)[:, None] * inv[None, :]
    hi = (jnp.arange(s // ROPE_LO) * ROPE_LO).astype(jnp.float32)[:, None] * inv[None, :]
    cl, sl = jnp.cos(lo)[None], jnp.sin(lo)[None]
    ch, sh = jnp.cos(hi)[:, None], jnp.sin(hi)[:, None]
    cos = (ch * cl - sh * sl).reshape(s, half)
    sin = (sh * cl + ch * sl).reshape(s, half)
    return jnp.concatenate([cos, cos], axis=1), jnp.concatenate([-sin, sin], axis=1)


def kernel(x, c, w_ada, b_ada, norm_mix_g, w_in, q_norm_g, k_norm_g, w_pool_grp, pool_scale, w_attn_br,
           w_pool_br, w_gate, b_gate, w_o, norm_ffn_g, w_up, conv_w, conv_b, w_down):
    batch, s, d = x.shape
    assert batch == 1 and d == D_MODEL and w_ada.shape[0] == 1
    bf = jnp.bfloat16
    row = lambda v: v.reshape(1, -1)
    cos_t, sin_t = _rope_tables(s)

    mod = _ada(c.reshape(d, 1), w_ada[0], row(b_ada[0]))
    sh1, sc1, g1, sh2, sc2, g2 = [mod[:, n * d:(n + 1) * d] for n in range(6)]

    x2 = x[0]
    proj = _proj(x2, row(norm_mix_g[0]), sh1, sc1, w_in[0].astype(bf), w_gate[0].astype(bf), row(b_gate[0]),
                 cos_t, sin_t, row(q_norm_g[0]), row(k_norm_g[0]))
    bias = _select(proj)
    attn = _attention(proj, bias)
    pooled = _pool(proj, w_pool_grp[0].astype(bf), row(pool_scale[0]))
    x1 = _merge(attn, pooled, proj, x2, g1, w_attn_br[0].astype(bf), w_pool_br[0].astype(bf),
                w_o[0].astype(bf))
    out = _ffn(x1, row(norm_ffn_g[0]), sh2, sc2, g2, w_up[0].astype(bf), conv_w[0], conv_b[0],
               w_down[0].astype(bf))
    return out[None]
```

```python
import functools
import math

import jax
import jax.numpy as jnp
from jax import lax
from jax.experimental import pallas as pl
from jax.experimental.pallas import tpu as pltpu

D_MODEL = 2048
N_HEADS = 8
HEAD_DIM = 128
ATTN_W = N_HEADS * HEAD_DIM
MOBA_BLOCK = 256
MOBA_TOPK = 3
POOL_WINDOWS = (2, 4, 8, 16)
POOL_W = 1024
POOL_GW = POOL_W // len(POOL_WINDOWS)
D_FF = 5632
CONV_W = 3
ROPE_THETA = 10000.0
EPS = 1e-6
NEG = -1e30
BLOCK_SHIFT = MOBA_BLOCK.bit_length() - 1
LOG2E = math.log2(math.e)

LANES = 128
SUBLANES = 8
PROJ_W = 4 * ATTN_W + 2 * D_MODEL
Q_COL, K_COL, V_COL, U_COL, G_COL = 0, ATTN_W, 2 * ATTN_W, 3 * ATTN_W, 4 * ATTN_W

MIB = 1024 * 1024


def _dot(a, b):
    return jnp.dot(a, b, preferred_element_type=jnp.float32)


def _dot_nt(a, b):
    return lax.dot_general(a, b, (((1,), (1,)), ((), ())), preferred_element_type=jnp.float32)


def _sigmoid(x):
    return 0.5 * jnp.tanh(0.5 * x) + 0.5


def _params(semantics, vmem_mib):
    return pltpu.CompilerParams(dimension_semantics=semantics, vmem_limit_bytes=vmem_mib * MIB)


NORM_ROWS = 64


def _norm_modulate_into(h_scr, x_ref, g_ref, shift_ref, scale_ref):
    gain = g_ref[...] * (1.0 + scale_ref[...])
    shift = shift_ref[...]

    def body(r, carry):
        rows = pl.ds(pl.multiple_of(r * NORM_ROWS, NORM_ROWS), NORM_ROWS)
        x = x_ref[rows, :]
        inv = lax.rsqrt(jnp.mean(x * x, axis=-1, keepdims=True) + EPS)
        h_scr[rows, :] = (x * inv * gain + shift).astype(h_scr.dtype)
        return carry

    lax.fori_loop(0, x_ref.shape[0] // NORM_ROWS, body, 0, unroll=4)


def _ada_kernel(c_ref, w_ref, b_ref, o_ref):
    c = c_ref[...]
    s = c * jax.nn.sigmoid(c)
    o_ref[...] = jnp.sum(s * w_ref[...], axis=0, keepdims=True) + b_ref[...]


def _ada(c_col, w_ada, b_ada):
    d, n = w_ada.shape
    tn = 1024
    return pl.pallas_call(
        _ada_kernel,
        out_shape=jax.ShapeDtypeStruct((1, n), jnp.float32),
        grid=(n // tn,),
        in_specs=[pl.BlockSpec((d, 1), lambda j: (0, 0)),
                  pl.BlockSpec((d, tn), lambda j: (0, j)),
                  pl.BlockSpec((1, tn), lambda j: (0, j))],
        out_specs=pl.BlockSpec((1, tn), lambda j: (0, j)),
        compiler_params=_params(("arbitrary",), 40),
        name="ada",
    )(c_col, w_ada, b_ada)


def _proj_kernel(x_ref, g_ref, sh_ref, sc_ref, wi_ref, wg_ref, cb_ref, cos_ref, sin_ref, qg_ref, kg_ref,
                 o_ref, h_scr, acc_scr, *, tn):
    j = pl.program_id(1)

    @pl.when(j == 0)
    def _():
        _norm_modulate_into(h_scr, x_ref, g_ref, sh_ref, sc_ref)

    @pl.when(j < G_COL // tn)
    def _():
        acc_scr[...] = _dot(h_scr[...], wi_ref[...])

    @pl.when(j >= G_COL // tn)
    def _():
        acc_scr[...] = _dot(h_scr[...], wg_ref[...])

    def head_norm_rope(gain_ref, out_scale):
        for hh in range(tn // HEAD_DIM):
            t = acc_scr[:, hh * HEAD_DIM:(hh + 1) * HEAD_DIM]
            ms = jnp.mean(t * t, axis=-1, keepdims=True)
            y = t * lax.rsqrt(ms + EPS) * gain_ref[...]
            r = y * cos_ref[...] + pltpu.roll(y, HEAD_DIM // 2, axis=1) * sin_ref[...]
            if out_scale != 1.0:
                r = r * out_scale
            o_ref[:, hh * HEAD_DIM:(hh + 1) * HEAD_DIM] = r.astype(o_ref.dtype)

    @pl.when(j < K_COL // tn)
    def _():
        head_norm_rope(qg_ref, HEAD_DIM ** -0.5 * LOG2E)

    @pl.when(jnp.logical_and(j >= K_COL // tn, j < V_COL // tn))
    def _():
        head_norm_rope(kg_ref, 1.0)

    @pl.when(jnp.logical_and(j >= V_COL // tn, j < G_COL // tn))
    def _():
        o_ref[...] = acc_scr[...].astype(o_ref.dtype)

    @pl.when(j >= G_COL // tn)
    def _():
        o_ref[...] = _sigmoid(acc_scr[...] + cb_ref[...]).astype(o_ref.dtype)


def _proj(x2, g, sh, sc, w_in, w_gate, b_gate, cos_t, sin_t, qg, kg):
    s, d = x2.shape
    tm, tn = 1024, 1024
    n_in = G_COL // tn
    vec = lambda: pl.BlockSpec((1, d), lambda i, j: (0, 0))
    return pl.pallas_call(
        functools.partial(_proj_kernel, tn=tn),
        out_shape=jax.ShapeDtypeStruct((s, PROJ_W), jnp.bfloat16),
        grid=(s // tm, PROJ_W // tn),
        in_specs=[pl.BlockSpec((tm, d), lambda i, j: (i, 0)),
                  vec(), vec(), vec(),
                  pl.BlockSpec((d, tn), lambda i, j: (0, jnp.minimum(j, n_in - 1))),
                  pl.BlockSpec((d, tn), lambda i, j: (0, jnp.maximum(j - n_in, 0))),
                  pl.BlockSpec((1, tn), lambda i, j: (0, jnp.maximum(j - n_in, 0))),
                  pl.BlockSpec((tm, HEAD_DIM), lambda i, j: (i, 0)),
                  pl.BlockSpec((tm, HEAD_DIM), lambda i, j: (i, 0)),
                  pl.BlockSpec((1, HEAD_DIM), lambda i, j: (0, 0)),
                  pl.BlockSpec((1, HEAD_DIM), lambda i, j: (0, 0))],
        out_specs=pl.BlockSpec((tm, tn), lambda i, j: (i, j)),
        scratch_shapes=[pltpu.VMEM((tm, d), jnp.bfloat16),
                        pltpu.VMEM((tm, tn), jnp.float32)],
        compiler_params=_params(("arbitrary", "arbitrary"), 56),
        name="proj",
    )(x2, g, sh, sc, w_in, w_gate, b_gate, cos_t, sin_t, qg, kg)


def _select_kernel(q_ref, k_ref, wu_ref, wd_ref, bias_ref, wu_out, wd_out, km_scr, *, tq):
    i = pl.program_id(1)
    nb = k_ref.shape[0] // MOBA_BLOCK
    wu_out[...] = wu_ref[...].astype(wu_out.dtype)
    wd_out[...] = wd_ref[...].astype(wd_out.dtype)

    @pl.when(i == 0)
    def _():
        k = k_ref[...].astype(jnp.float32).reshape(nb, MOBA_BLOCK, HEAD_DIM)
        km = jnp.sum(k, axis=1) * (1.0 / MOBA_BLOCK)
        p0 = km.astype(jnp.bfloat16)
        r1 = km - p0.astype(jnp.float32)
        p1 = r1.astype(jnp.bfloat16)
        p2 = (r1 - p1.astype(jnp.float32)).astype(jnp.bfloat16)
        km_scr[0] = p0
        km_scr[1] = p1
        km_scr[2] = p2

    q = q_ref[...]
    gate = _dot_nt(km_scr[0], q) + _dot_nt(km_scr[1], q) + _dot_nt(km_scr[2], q)
    blk = lax.broadcasted_iota(jnp.int32, gate.shape, 0)
    qpos = i * tq + lax.broadcasted_iota(jnp.int32, gate.shape, 1)
    own = lax.shift_right_logical(qpos, BLOCK_SHIFT)
    past = blk < own
    g = jnp.where(past, gate, NEG)
    bias = jnp.where(blk == own, 0.0, NEG)
    for _ in range(MOBA_TOPK):
        top = jnp.max(g, axis=0, keepdims=True)
        first = jnp.min(jnp.where(g == top, blk, nb), axis=0, keepdims=True)
        pick = blk == first
        bias = jnp.where(jnp.logical_and(pick, past), 0.0, bias)
        g = jnp.where(pick, -jnp.inf, g)
    bias_ref[...] = bias


def _eye(n):
    return (lax.broadcasted_iota(jnp.int32, (n, n), 0)
            == lax.broadcasted_iota(jnp.int32, (n, n), 1)).astype(jnp.bfloat16)


def _select(proj, w_up, w_down):
    s = proj.shape[0]
    tq = 4096
    kcol = K_COL // HEAD_DIM
    n_i = s // tq
    steps = N_HEADS * n_i
    wu = w_up
    wd = w_down.reshape(-1, w_up.shape[1])
    slab = lambda w: pl.BlockSpec((w.shape[0] // steps, w.shape[1]), lambda h, i: (h * n_i + i, 0))
    bias, wu_bf, wd_bf = pl.pallas_call(
        functools.partial(_select_kernel, tq=tq),
        out_shape=[jax.ShapeDtypeStruct((N_HEADS, s // MOBA_BLOCK, s), jnp.float32),
                   jax.ShapeDtypeStruct(wu.shape, jnp.bfloat16),
                   jax.ShapeDtypeStruct(wd.shape, jnp.bfloat16)],
        grid=(N_HEADS, n_i),
        in_specs=[pl.BlockSpec((tq, HEAD_DIM), lambda h, i: (i, h)),
                  pl.BlockSpec((s, HEAD_DIM), lambda h, i: (0, kcol + h)),
                  slab(wu), slab(wd)],
        out_specs=[pl.BlockSpec((None, s // MOBA_BLOCK, tq), lambda h, i: (h, 0, i)),
                   slab(wu), slab(wd)],
        scratch_shapes=[pltpu.VMEM((3, s // MOBA_BLOCK, HEAD_DIM), jnp.bfloat16)],
        compiler_params=_params(("arbitrary", "arbitrary"), 48),
        name="select",
    )(proj, proj, wu, wd)
    return bias, wu_bf, wd_bf.reshape(w_down.shape)


ATTN_QB = 8
ATTN_UNROLL = 8
ONES_ROWS = 16
XPOSE_CHUNK = 2048


def _attn_kernel(q_ref, bias_ref, k_ref, v_ref, o_ref, vaug_scr, acc_scr, s_scr, p_scr, ptail_scr):
    i = pl.program_id(1)
    s_len = k_ref.shape[0]
    blk = MOBA_BLOCK

    @pl.when(i == 0)
    def _():
        for c in range(s_len // XPOSE_CHUNK):
            keys = slice(c * XPOSE_CHUNK, (c + 1) * XPOSE_CHUNK)
            vaug_scr[:HEAD_DIM, keys] = _dot_nt(_eye(HEAD_DIM), v_ref[keys, :]).astype(vaug_scr.dtype)
        vaug_scr[HEAD_DIM:, :] = jnp.ones((ONES_ROWS, s_len), vaug_scr.dtype)

    ng = ATTN_QB
    n_past = i * ng
    groups = [slice(g * blk, (g + 1) * blk) for g in range(ng)]
    key_le_query = (lax.broadcasted_iota(jnp.int32, (blk, blk), 0)
                    <= lax.broadcasted_iota(jnp.int32, (blk, blk), 1))

    def score_tile(j, slot, g, causal):
        kb = k_ref[pl.ds(pl.multiple_of(j * blk, blk), blk), :]
        s_t = _dot_nt(kb, q_ref[groups[g], :])
        if causal:
            s_t = jnp.where(key_le_query, s_t, NEG)
        s_scr[slot, :, groups[g]] = s_t
        return jnp.max(s_t, axis=0, keepdims=True)

    def softmax_tile(j, slot, g, m_old, mx):
        b = bias_ref[pl.ds(j, 1), groups[g]]
        m_new = jnp.maximum(m_old, mx + b)
        shift = jnp.maximum(m_new - b, mx)
        p_scr[slot, :, groups[g]] = jnp.exp2(s_scr[slot, :, groups[g]] - shift).astype(p_scr.dtype)
        return m_new, jnp.exp2(m_old - m_new)

    def value_tile(j, slot, g, alpha):
        vb = vaug_scr[:, pl.ds(pl.multiple_of(j * blk, blk), blk)]
        acc_scr[:, groups[g]] = alpha * acc_scr[:, groups[g]] + _dot(vb, p_scr[slot, :, groups[g]])

    def past_step(j, slot, m, mx, a_prev):
        mx_next, m_alpha = [], []
        for g in range(ng):
            value_tile(jnp.maximum(j - 1, 0), 1 - slot, g, a_prev[g])
            mx_next.append(score_tile(j + 1, 1 - slot, g, False))
            m_alpha.append(softmax_tile(j, slot, g, m[g], mx[g]))
        mx_next = tuple(mx_next)
        return tuple(ma[0] for ma in m_alpha), mx_next, tuple(ma[1] for ma in m_alpha)

    def body(jj, carry):
        for u in range(ATTN_UNROLL):
            carry = past_step(ATTN_UNROLL * jj + u, u % 2, *carry)
        return carry

    acc_scr[...] = jnp.zeros_like(acc_scr)
    p_scr[1] = jnp.zeros(p_scr.shape[1:], p_scr.dtype)
    init = (tuple(jnp.full((1, blk), NEG, jnp.float32) for _ in range(ng)),
            tuple(score_tile(0, 0, g, False) for g in range(ng)),
            tuple(jnp.ones((1, blk), jnp.float32) for _ in range(ng)))
    m, mx, alpha = lax.fori_loop(0, n_past // ATTN_UNROLL, body, init)
    m, mx, alpha = list(m), list(mx), list(alpha)

    for g in range(ng):
        value_tile(jnp.maximum(n_past - 1, 0), 1, g, alpha[g])
    mxs = {(g, 0): mx[g] for g in range(1, ng)}
    mxs[(0, 0)] = score_tile(n_past, 0, 0, True)
    for t in range(1, ng):
        for g in range(t, ng):
            mxs[(g, t)] = score_tile(n_past + t, t, g, g == t)
    span0 = pl.multiple_of(n_past * blk, blk)
    for g in range(ng):
        bs = [bias_ref[pl.ds(n_past + t, 1), groups[g]] for t in range(g + 1)]
        m_new = m[g]
        for t in range(g + 1):
            m_new = jnp.maximum(m_new, mxs[(g, t)] + bs[t])
        for t in range(g + 1):
            shift = jnp.maximum(m_new - bs[t], mxs[(g, t)])
            ptail_scr[t * blk:(t + 1) * blk, groups[g]] = (
                jnp.exp2(s_scr[t, :, groups[g]] - shift).astype(ptail_scr.dtype))
        pv = _dot(vaug_scr[:, pl.ds(span0, (g + 1) * blk)], ptail_scr[:(g + 1) * blk, groups[g]])
        acc_scr[:, groups[g]] = jnp.exp2(m[g] - m_new) * acc_scr[:, groups[g]] + pv
    o_ref[...] = (acc_scr[:HEAD_DIM, :] / acc_scr[HEAD_DIM:HEAD_DIM + 1, :]).T.astype(o_ref.dtype)


def _attention(proj, bias):
    s = proj.shape[0]
    tq = ATTN_QB * MOBA_BLOCK
    kcol, vcol = K_COL // HEAD_DIM, V_COL // HEAD_DIM
    return pl.pallas_call(
        _attn_kernel,
        out_shape=jax.ShapeDtypeStruct((s, ATTN_W), jnp.bfloat16),
        grid=(N_HEADS, s // tq),
        in_specs=[pl.BlockSpec((tq, HEAD_DIM), lambda h, i: (i, h)),
                  pl.BlockSpec((None, s // MOBA_BLOCK, tq), lambda h, i: (h, 0, i)),
                  pl.BlockSpec((s, HEAD_DIM), lambda h, i: (0, kcol + h)),
                  pl.BlockSpec((s, HEAD_DIM), lambda h, i: (0, vcol + h))],
        out_specs=pl.BlockSpec((tq, HEAD_DIM), lambda h, i: (i, h)),
        scratch_shapes=[pltpu.VMEM((HEAD_DIM + ONES_ROWS, s), jnp.bfloat16),
                        pltpu.VMEM((HEAD_DIM + ONES_ROWS, tq), jnp.float32),
                        pltpu.VMEM((ATTN_QB, MOBA_BLOCK, tq), jnp.float32),
                        pltpu.VMEM((2, MOBA_BLOCK, tq), jnp.bfloat16),
                        pltpu.VMEM((tq, tq), jnp.bfloat16)],
        compiler_params=_params(("arbitrary", "arbitrary"), 56),
        name="attention",
    )(proj, bias, proj, proj)


POOL_HALO = 16


def _pool_kernel(u_ref, halo_ref, w_ref, ls_ref, o_ref):
    i = pl.program_id(0)
    tm = u_ref.shape[0]
    u = u_ref[...].astype(jnp.float32)
    halo = jnp.where(i == 0, 0.0, halo_ref[...].astype(jnp.float32))
    t = i * tm + lax.broadcasted_iota(jnp.int32, (tm, 1), 0)
    for g, win in enumerate(POOL_WINDOWS):
        cols = slice(g * POOL_GW, (g + 1) * POOL_GW)
        ug = u[:, cols]
        e = jnp.concatenate([halo[:, cols], ug], axis=0)
        span = 1
        while span < win:
            e = e + pltpu.roll(e, span, axis=0)
            span *= 2
        cnt = jnp.minimum(t + 1, win).astype(jnp.float32)
        dg = e[POOL_HALO:, :] / cnt - ug
        mixed = _dot(dg.astype(jnp.bfloat16), w_ref[g])
        o_ref[:, cols] = (mixed * ls_ref[:, cols]).astype(o_ref.dtype)


def _pool(proj, w_grp, ls):
    s = proj.shape[0]
    tm = 1024
    ucol = U_COL // POOL_W
    return pl.pallas_call(
        _pool_kernel,
        out_shape=jax.ShapeDtypeStruct((s, POOL_W), jnp.bfloat16),
        grid=(s // tm,),
        in_specs=[pl.BlockSpec((tm, POOL_W), lambda i: (i, ucol)),
                  pl.BlockSpec((POOL_HALO, POOL_W),
                               lambda i: (jnp.maximum(i * (tm // POOL_HALO) - 1, 0), ucol)),
                  pl.BlockSpec(w_grp.shape, lambda i: (0, 0, 0)),
                  pl.BlockSpec((1, POOL_W), lambda i: (0, 0))],
        out_specs=pl.BlockSpec((tm, POOL_W), lambda i: (i, 0)),
        compiler_params=_params(("arbitrary",), 48),
        name="pool",
    )(proj, proj, w_grp, ls)


def _merge_kernel(a_ref, p_ref, ga_ref, gp_ref, x_ref, g1_ref, wa_ref, wp_ref, wo_ref, o_ref):
    j = pl.program_id(1)

    @pl.when(j == 0)
    def _():
        o_ref[...] = jnp.zeros_like(o_ref)

    ya = _dot(a_ref[...], wa_ref[...])
    yp = _dot(p_ref[...], wp_ref[...])
    merged = ga_ref[...].astype(jnp.float32) * ya + gp_ref[...].astype(jnp.float32) * yp
    o_ref[...] += _dot(merged.astype(jnp.bfloat16), wo_ref[...])

    @pl.when(j == pl.num_programs(1) - 1)
    def _():
        o_ref[...] = x_ref[...] + g1_ref[...] * o_ref[...]


def _merge(attn, pooled, proj, x2, g1, wa, wp, wo):
    s, d = x2.shape
    tm, tn = 512, 1024
    ga0, gp0 = G_COL // tn, (G_COL + d) // tn
    return pl.pallas_call(
        _merge_kernel,
        out_shape=jax.ShapeDtypeStruct((s, d), jnp.float32),
        grid=(s // tm, d // tn),
        in_specs=[pl.BlockSpec((tm, ATTN_W), lambda i, j: (i, 0)),
                  pl.BlockSpec((tm, POOL_W), lambda i, j: (i, 0)),
                  pl.BlockSpec((tm, tn), lambda i, j: (i, ga0 + j)),
                  pl.BlockSpec((tm, tn), lambda i, j: (i, gp0 + j)),
                  pl.BlockSpec((tm, d), lambda i, j: (i, 0)),
                  pl.BlockSpec((1, d), lambda i, j: (0, 0)),
                  pl.BlockSpec((ATTN_W, tn), lambda i, j: (0, j)),
                  pl.BlockSpec((POOL_W, tn), lambda i, j: (0, j)),
                  pl.BlockSpec((tn, d), lambda i, j: (j, 0))],
        out_specs=pl.BlockSpec((tm, d), lambda i, j: (i, 0)),
        compiler_params=_params(("arbitrary", "arbitrary"), 56),
        name="merge",
    )(attn, pooled, proj, proj, x2, g1, wa, wp, wo)


FFN_CHUNK = 256


def _ffn_kernel(x_ref, g_ref, sh_ref, sc_ref, g2_ref, wua_ref, wub_ref, cw_ref, cb_ref, wdp_ref, wd_ref,
                o_ref, h_scr, tail_scr, upa_scr, upb_scr):
    i = pl.program_id(0)
    j = pl.program_id(1)
    tm = x_ref.shape[0]
    n_chunks = cw_ref.shape[0] // 2

    @pl.when(j == 0)
    def _():
        _norm_modulate_into(h_scr, x_ref, g_ref, sh_ref, sc_ref)
        o_ref[...] = jnp.zeros_like(o_ref)

    @pl.when(i == 0)
    def _():
        tail_scr[j] = jnp.zeros(tail_scr.shape[1:], tail_scr.dtype)

    chunks = [slice(c * FFN_CHUNK, (c + 1) * FFN_CHUNK) for c in range(2)]

    def up_stage(c):
        for half, (w_ref, u_scr) in enumerate(((wua_ref, upa_scr), (wub_ref, upb_scr))):
            u_scr[c, :SUBLANES, :] = tail_scr[j, half, :, chunks[c]]
            up = _dot(h_scr[...], w_ref[:, chunks[c]])
            u_scr[c, SUBLANES:, :] = up
            tail_scr[j, half, :, chunks[c]] = up[tm - SUBLANES:, :]

    def conv(ext, n):
        cw = cw_ref[n]
        out = cb_ref[n] + cw[CONV_W - 1:CONV_W, :] * ext[SUBLANES:, :]
        for back in range(1, CONV_W):
            tap = CONV_W - 1 - back
            out = out + cw[tap:tap + 1, :] * pltpu.roll(ext, back, axis=0)[SUBLANES:, :]
        return out

    def gated(c, n):
        a = conv(upa_scr[c], n)
        b = conv(upb_scr[c], n_chunks + n)
        return (a * _sigmoid(a) * b).astype(jnp.bfloat16)

    up_stage(0)
    act_prev = gated(1, jnp.maximum(2 * j - 1, 0))
    act_prev = jnp.where(j > 0, act_prev, jnp.zeros_like(act_prev))
    up_stage(1)
    act_0 = gated(0, 2 * j)
    o_ref[...] += _dot(act_prev, wdp_ref[chunks[1], :]) + _dot(act_0, wd_ref[chunks[0], :])

    @pl.when(j == pl.num_programs(1) - 1)
    def _():
        o_ref[...] += _dot(gated(1, 2 * j + 1), wd_ref[chunks[1], :])
        o_ref[...] = x_ref[...] + g2_ref[...] * o_ref[...]


def _ffn(x1, g, sh, sc, g2, w_up, conv_w, conv_b, w_down):
    s, d = x1.shape
    tm, tf = 512, 2 * FFN_CHUNK
    nf = D_FF // tf
    n_chunks = 2 * D_FF // FFN_CHUNK
    cw = conv_w.reshape(CONV_W, n_chunks, FFN_CHUNK).transpose(1, 0, 2)
    cb = conv_b.reshape(n_chunks, 1, FFN_CHUNK)
    vec = lambda: pl.BlockSpec((1, d), lambda i, j: (0, 0))
    return pl.pallas_call(
        _ffn_kernel,
        out_shape=jax.ShapeDtypeStruct((s, d), jnp.float32),
        grid=(s // tm, nf),
        in_specs=[pl.BlockSpec((tm, d), lambda i, j: (i, 0)),
                  vec(), vec(), vec(), vec(),
                  pl.BlockSpec((d, tf), lambda i, j: (0, j)),
                  pl.BlockSpec((d, tf), lambda i, j: (0, nf + j)),
                  pl.BlockSpec(cw.shape, lambda i, j: (0, 0, 0)),
                  pl.BlockSpec(cb.shape, lambda i, j: (0, 0, 0)),
                  pl.BlockSpec((tf, d), lambda i, j: (jnp.maximum(j - 1, 0), 0)),
                  pl.BlockSpec((tf, d), lambda i, j: (j, 0))],
        out_specs=pl.BlockSpec((tm, d), lambda i, j: (i, 0)),
        scratch_shapes=[pltpu.VMEM((tm, d), jnp.bfloat16),
                        pltpu.VMEM((nf, 2, SUBLANES, tf), jnp.float32),
                        pltpu.VMEM((2, SUBLANES + tm, FFN_CHUNK), jnp.float32),
                        pltpu.VMEM((2, SUBLANES + tm, FFN_CHUNK), jnp.float32)],
        compiler_params=_params(("arbitrary", "arbitrary"), 48),
        name="ffn",
    )(x1, g, sh, sc, g2, w_up, w_up, cw, cb, w_down, w_down)


ROPE_LO = 128


def _rope_tables(s):
    half = HEAD_DIM // 2
    inv = ROPE_THETA ** (-jnp.arange(half, dtype=jnp.float32) / half)
    lo = jnp.arange(ROPE_LO).astype(jnp.float32)[:, None] * inv[None, :]
    hi = (jnp.arange(s // ROPE_LO) * ROPE_LO).astype(jnp.float32)[:, None] * inv[None, :]
    cl, sl = jnp.cos(lo)[None], jnp.sin(lo)[None]
    ch, sh = jnp.cos(hi)[:, None], jnp.sin(hi)[:, None]
    cos = (ch * cl - sh * sl).reshape(s, half)
    sin = (sh * cl + ch * sl).reshape(s, half)
    return jnp.concatenate([cos, cos], axis=1), jnp.concatenate([-sin, sin], axis=1)


def kernel(x, c, w_ada, b_ada, norm_mix_g, w_in, q_norm_g, k_norm_g, w_pool_grp, pool_scale, w_attn_br,
           w_pool_br, w_gate, b_gate, w_o, norm_ffn_g, w_up, conv_w, conv_b, w_down):
    batch, s, d = x.shape
    assert batch == 1 and d == D_MODEL and w_ada.shape[0] == 1
    bf = jnp.bfloat16
    row = lambda v: v.reshape(1, -1)
    cos_t, sin_t = _rope_tables(s)

    mod = _ada(c.reshape(d, 1), w_ada[0], row(b_ada[0]))
    sh1, sc1, g1, sh2, sc2, g2 = [mod[:, n * d:(n + 1) * d] for n in range(6)]

    x2 = x[0]
    proj = _proj(x2, row(norm_mix_g[0]), sh1, sc1, w_in[0].astype(bf), w_gate[0].astype(bf), row(b_gate[0]),
                 cos_t, sin_t, row(q_norm_g[0]), row(k_norm_g[0]))
    bias, w_up_bf, w_down_bf = _select(proj, w_up[0], w_down[0])
    attn = _attention(proj, bias)
    pooled = _pool(proj, w_pool_grp[0].astype(bf), row(pool_scale[0]))
    x1 = _merge(attn, pooled, proj, x2, g1, w_attn_br[0].astype(bf), w_pool_br[0].astype(bf),
                w_o[0].astype(bf))
    out = _ffn(x1, row(norm_ffn_g[0]), sh2, sc2, g2, w_up_bf, conv_w[0], conv_b[0], w_down_bf)
    return out[None]
```

```python
import functools
import math

import jax
import jax.numpy as jnp
from jax import lax
from jax.experimental import pallas as pl
from jax.experimental.pallas import tpu as pltpu

D_MODEL = 2048
N_HEADS = 8
HEAD_DIM = 128
ATTN_W = N_HEADS * HEAD_DIM
MOBA_BLOCK = 256
MOBA_TOPK = 3
POOL_WINDOWS = (2, 4, 8, 16)
POOL_W = 1024
POOL_GW = POOL_W // len(POOL_WINDOWS)
D_FF = 5632
CONV_W = 3
ROPE_THETA = 10000.0
EPS = 1e-6
NEG = -1e30
BLOCK_SHIFT = MOBA_BLOCK.bit_length() - 1
LOG2E = math.log2(math.e)

LANES = 128
SUBLANES = 8
PROJ_W = 4 * ATTN_W + 2 * D_MODEL
Q_COL, K_COL, V_COL, U_COL, G_COL = 0, ATTN_W, 2 * ATTN_W, 3 * ATTN_W, 4 * ATTN_W

MIB = 1024 * 1024


def _dot(a, b):
    return jnp.dot(a, b, preferred_element_type=jnp.float32)


def _dot_nt(a, b):
    return lax.dot_general(a, b, (((1,), (1,)), ((), ())), preferred_element_type=jnp.float32)


def _sigmoid(x):
    return 0.5 * jnp.tanh(0.5 * x) + 0.5


def _params(semantics, vmem_mib):
    return pltpu.CompilerParams(dimension_semantics=semantics, vmem_limit_bytes=vmem_mib * MIB)


NORM_ROWS = 64


def _norm_modulate_into(h_scr, x_ref, g_ref, shift_ref, scale_ref):
    gain = g_ref[...] * (1.0 + scale_ref[...])
    shift = shift_ref[...]

    def body(r, carry):
        rows = pl.ds(pl.multiple_of(r * NORM_ROWS, NORM_ROWS), NORM_ROWS)
        x = x_ref[rows, :]
        inv = lax.rsqrt(jnp.mean(x * x, axis=-1, keepdims=True) + EPS)
        h_scr[rows, :] = (x * inv * gain + shift).astype(h_scr.dtype)
        return carry

    lax.fori_loop(0, x_ref.shape[0] // NORM_ROWS, body, 0, unroll=4)


def _ada_kernel(c_ref, w_ref, b_ref, o_ref):
    c = c_ref[...]
    s = c * jax.nn.sigmoid(c)
    o_ref[...] = jnp.sum(s * w_ref[...], axis=0, keepdims=True) + b_ref[...]


def _ada(c_col, w_ada, b_ada):
    d, n = w_ada.shape
    tn = 1024
    return pl.pallas_call(
        _ada_kernel,
        out_shape=jax.ShapeDtypeStruct((1, n), jnp.float32),
        grid=(n // tn,),
        in_specs=[pl.BlockSpec((d, 1), lambda j: (0, 0)),
                  pl.BlockSpec((d, tn), lambda j: (0, j)),
                  pl.BlockSpec((1, tn), lambda j: (0, j))],
        out_specs=pl.BlockSpec((1, tn), lambda j: (0, j)),
        compiler_params=_params(("arbitrary",), 40),
        name="ada",
    )(c_col, w_ada, b_ada)


def _proj_kernel(x_ref, g_ref, sh_ref, sc_ref, wi_ref, wg_ref, cb_ref, cos_ref, sin_ref, qg_ref, kg_ref,
                 o_ref, h_scr, acc_scr, *, tn):
    j = pl.program_id(1)

    @pl.when(j == 0)
    def _():
        _norm_modulate_into(h_scr, x_ref, g_ref, sh_ref, sc_ref)

    @pl.when(j < G_COL // tn)
    def _():
        acc_scr[...] = _dot(h_scr[...], wi_ref[...])

    @pl.when(j >= G_COL // tn)
    def _():
        acc_scr[...] = _dot(h_scr[...], wg_ref[...])

    def head_norm_rope(gain_ref, out_scale):
        for hh in range(tn // HEAD_DIM):
            t = acc_scr[:, hh * HEAD_DIM:(hh + 1) * HEAD_DIM]
            ms = jnp.mean(t * t, axis=-1, keepdims=True)
            y = t * lax.rsqrt(ms + EPS) * gain_ref[...]
            r = y * cos_ref[...] + pltpu.roll(y, HEAD_DIM // 2, axis=1) * sin_ref[...]
            if out_scale != 1.0:
                r = r * out_scale
            o_ref[:, hh * HEAD_DIM:(hh + 1) * HEAD_DIM] = r.astype(o_ref.dtype)

    @pl.when(j < K_COL // tn)
    def _():
        head_norm_rope(qg_ref, HEAD_DIM ** -0.5 * LOG2E)

    @pl.when(jnp.logical_and(j >= K_COL // tn, j < V_COL // tn))
    def _():
        head_norm_rope(kg_ref, 1.0)

    @pl.when(jnp.logical_and(j >= V_COL // tn, j < G_COL // tn))
    def _():
        o_ref[...] = acc_scr[...].astype(o_ref.dtype)

    @pl.when(j >= G_COL // tn)
    def _():
        o_ref[...] = _sigmoid(acc_scr[...] + cb_ref[...]).astype(o_ref.dtype)


def _proj(x2, g, sh, sc, w_in, w_gate, b_gate, cos_t, sin_t, qg, kg):
    s, d = x2.shape
    tm, tn = 1024, 1024
    n_in = G_COL // tn
    vec = lambda: pl.BlockSpec((1, d), lambda i, j: (0, 0))
    return pl.pallas_call(
        functools.partial(_proj_kernel, tn=tn),
        out_shape=jax.ShapeDtypeStruct((s, PROJ_W), jnp.bfloat16),
        grid=(s // tm, PROJ_W // tn),
        in_specs=[pl.BlockSpec((tm, d), lambda i, j: (i, 0)),
                  vec(), vec(), vec(),
                  pl.BlockSpec((d, tn), lambda i, j: (0, jnp.minimum(j, n_in - 1))),
                  pl.BlockSpec((d, tn), lambda i, j: (0, jnp.maximum(j - n_in, 0))),
                  pl.BlockSpec((1, tn), lambda i, j: (0, jnp.maximum(j - n_in, 0))),
                  pl.BlockSpec((tm, HEAD_DIM), lambda i, j: (i, 0)),
                  pl.BlockSpec((tm, HEAD_DIM), lambda i, j: (i, 0)),
                  pl.BlockSpec((1, HEAD_DIM), lambda i, j: (0, 0)),
                  pl.BlockSpec((1, HEAD_DIM), lambda i, j: (0, 0))],
        out_specs=pl.BlockSpec((tm, tn), lambda i, j: (i, j)),
        scratch_shapes=[pltpu.VMEM((tm, d), jnp.bfloat16),
                        pltpu.VMEM((tm, tn), jnp.float32)],
        compiler_params=_params(("arbitrary", "arbitrary"), 56),
        name="proj",
    )(x2, g, sh, sc, w_in, w_gate, b_gate, cos_t, sin_t, qg, kg)


def _select_kernel(q_ref, k_ref, wu_ref, wd_ref, bias_ref, wu_out, wd_out, km_scr, *, tq):
    i = pl.program_id(1)
    nb = k_ref.shape[0] // MOBA_BLOCK
    wu_out[...] = wu_ref[...].astype(wu_out.dtype)
    wd_out[...] = wd_ref[...].astype(wd_out.dtype)

    @pl.when(i == 0)
    def _():
        k = k_ref[...].astype(jnp.float32).reshape(nb, MOBA_BLOCK, HEAD_DIM)
        km = jnp.sum(k, axis=1) * (1.0 / MOBA_BLOCK)
        p0 = km.astype(jnp.bfloat16)
        r1 = km - p0.astype(jnp.float32)
        p1 = r1.astype(jnp.bfloat16)
        p2 = (r1 - p1.astype(jnp.float32)).astype(jnp.bfloat16)
        km_scr[0] = p0
        km_scr[1] = p1
        km_scr[2] = p2

    q = q_ref[...]
    gate = _dot_nt(km_scr[0], q) + _dot_nt(km_scr[1], q) + _dot_nt(km_scr[2], q)
    blk = lax.broadcasted_iota(jnp.int32, gate.shape, 0)
    qpos = i * tq + lax.broadcasted_iota(jnp.int32, gate.shape, 1)
    own = lax.shift_right_logical(qpos, BLOCK_SHIFT)
    past = blk < own
    g = jnp.where(past, gate, NEG)
    bias = jnp.where(blk == own, 0.0, NEG)
    for _ in range(MOBA_TOPK):
        top = jnp.max(g, axis=0, keepdims=True)
        first = jnp.min(jnp.where(g == top, blk, nb), axis=0, keepdims=True)
        pick = blk == first
        bias = jnp.where(jnp.logical_and(pick, past), 0.0, bias)
        g = jnp.where(pick, -jnp.inf, g)
    bias_ref[...] = bias


def _eye(n):
    return (lax.broadcasted_iota(jnp.int32, (n, n), 0)
            == lax.broadcasted_iota(jnp.int32, (n, n), 1)).astype(jnp.bfloat16)


def _select(proj, w_up, w_down):
    s = proj.shape[0]
    tq = 4096
    kcol = K_COL // HEAD_DIM
    n_i = s // tq
    steps = N_HEADS * n_i

    def slab(w, lead):
        rows, cols = w.shape[-2] // steps, w.shape[-1]
        if lead:
            return pl.BlockSpec((None, rows, cols), lambda h, i: (0, h * n_i + i, 0))
        return pl.BlockSpec((rows, cols), lambda h, i: (h * n_i + i, 0))

    return pl.pallas_call(
        functools.partial(_select_kernel, tq=tq),
        out_shape=[jax.ShapeDtypeStruct((N_HEADS, s // MOBA_BLOCK, s), jnp.float32),
                   jax.ShapeDtypeStruct(w_up.shape[1:], jnp.bfloat16),
                   jax.ShapeDtypeStruct(w_down.shape[1:], jnp.bfloat16)],
        grid=(N_HEADS, n_i),
        in_specs=[pl.BlockSpec((tq, HEAD_DIM), lambda h, i: (i, h)),
                  pl.BlockSpec((s, HEAD_DIM), lambda h, i: (0, kcol + h)),
                  slab(w_up, True), slab(w_down, True)],
        out_specs=[pl.BlockSpec((None, s // MOBA_BLOCK, tq), lambda h, i: (h, 0, i)),
                   slab(w_up, False), slab(w_down, False)],
        scratch_shapes=[pltpu.VMEM((3, s // MOBA_BLOCK, HEAD_DIM), jnp.bfloat16)],
        compiler_params=_params(("arbitrary", "arbitrary"), 48),
        name="select",
    )(proj, proj, w_up, w_down)


ATTN_QB = 8
ATTN_UNROLL = 8
ONES_ROWS = 16
XPOSE_CHUNK = 2048


def _attn_kernel(q_ref, bias_ref, k_ref, v_ref, o_ref, vaug_scr, acc_scr, s_scr, p_scr, ptail_scr):
    i = pl.program_id(1)
    s_len = k_ref.shape[0]
    blk = MOBA_BLOCK

    @pl.when(i == 0)
    def _():
        for c in range(s_len // XPOSE_CHUNK):
            keys = slice(c * XPOSE_CHUNK, (c + 1) * XPOSE_CHUNK)
            vaug_scr[:HEAD_DIM, keys] = _dot_nt(_eye(HEAD_DIM), v_ref[keys, :]).astype(vaug_scr.dtype)
        vaug_scr[HEAD_DIM:, :] = jnp.ones((ONES_ROWS, s_len), vaug_scr.dtype)

    ng = ATTN_QB
    n_past = i * ng
    groups = [slice(g * blk, (g + 1) * blk) for g in range(ng)]
    key_le_query = (lax.broadcasted_iota(jnp.int32, (blk, blk), 0)
                    <= lax.broadcasted_iota(jnp.int32, (blk, blk), 1))

    def score_tile(j, slot, g, causal):
        kb = k_ref[pl.ds(pl.multiple_of(j * blk, blk), blk), :]
        s_t = _dot_nt(kb, q_ref[groups[g], :])
        if causal:
            s_t = jnp.where(key_le_query, s_t, NEG)
        s_scr[slot, :, groups[g]] = s_t
        return jnp.max(s_t, axis=0, keepdims=True)

    def softmax_tile(j, slot, g, m_old, mx):
        b = bias_ref[pl.ds(j, 1), groups[g]]
        m_new = jnp.maximum(m_old, mx + b)
        shift = jnp.maximum(m_new - b, mx)
        p_scr[slot, :, groups[g]] = jnp.exp2(s_scr[slot, :, groups[g]] - shift).astype(p_scr.dtype)
        return m_new, jnp.exp2(m_old - m_new)

    def value_tile(j, slot, g, alpha):
        vb = vaug_scr[:, pl.ds(pl.multiple_of(j * blk, blk), blk)]
        acc_scr[:, groups[g]] = alpha * acc_scr[:, groups[g]] + _dot(vb, p_scr[slot, :, groups[g]])

    def past_step(j, slot, m, mx, a_prev):
        mx_next, m_alpha = [], []
        for g in range(ng):
            value_tile(jnp.maximum(j - 1, 0), 1 - slot, g, a_prev[g])
            mx_next.append(score_tile(j + 1, 1 - slot, g, False))
            m_alpha.append(softmax_tile(j, slot, g, m[g], mx[g]))
        mx_next = tuple(mx_next)
        return tuple(ma[0] for ma in m_alpha), mx_next, tuple(ma[1] for ma in m_alpha)

    def body(jj, carry):
        for u in range(ATTN_UNROLL):
            carry = past_step(ATTN_UNROLL * jj + u, u % 2, *carry)
        return carry

    acc_scr[...] = jnp.zeros_like(acc_scr)
    p_scr[1] = jnp.zeros(p_scr.shape[1:], p_scr.dtype)
    init = (tuple(jnp.full((1, blk), NEG, jnp.float32) for _ in range(ng)),
            tuple(score_tile(0, 0, g, False) for g in range(ng)),
            tuple(jnp.ones((1, blk), jnp.float32) for _ in range(ng)))
    m, mx, alpha = lax.fori_loop(0, n_past // ATTN_UNROLL, body, init)
    m, mx, alpha = list(m), list(mx), list(alpha)

    for g in range(ng):
        value_tile(jnp.maximum(n_past - 1, 0), 1, g, alpha[g])
    mxs = {(g, 0): mx[g] for g in range(1, ng)}
    mxs[(0, 0)] = score_tile(n_past, 0, 0, True)
    for t in range(1, ng):
        for g in range(t, ng):
            mxs[(g, t)] = score_tile(n_past + t, t, g, g == t)
    span0 = pl.multiple_of(n_past * blk, blk)
    for g in range(ng):
        bs = [bias_ref[pl.ds(n_past + t, 1), groups[g]] for t in range(g + 1)]
        m_new = m[g]
        for t in range(g + 1):
            m_new = jnp.maximum(m_new, mxs[(g, t)] + bs[t])
        for t in range(g + 1):
            shift = jnp.maximum(m_new - bs[t], mxs[(g, t)])
            ptail_scr[t * blk:(t + 1) * blk, groups[g]] = (
                jnp.exp2(s_scr[t, :, groups[g]] - shift).astype(ptail_scr.dtype))
        pv = _dot(vaug_scr[:, pl.ds(span0, (g + 1) * blk)], ptail_scr[:(g + 1) * blk, groups[g]])
        acc_scr[:, groups[g]] = jnp.exp2(m[g] - m_new) * acc_scr[:, groups[g]] + pv
    o_ref[...] = (acc_scr[:HEAD_DIM, :] / acc_scr[HEAD_DIM:HEAD_DIM + 1, :]).T.astype(o_ref.dtype)


def _attention(proj, bias):
    s = proj.shape[0]
    tq = ATTN_QB * MOBA_BLOCK
    kcol, vcol = K_COL // HEAD_DIM, V_COL // HEAD_DIM
    return pl.pallas_call(
        _attn_kernel,
        out_shape=jax.ShapeDtypeStruct((s, ATTN_W), jnp.bfloat16),
        grid=(N_HEADS, s // tq),
        in_specs=[pl.BlockSpec((tq, HEAD_DIM), lambda h, i: (i, h)),
                  pl.BlockSpec((None, s // MOBA_BLOCK, tq), lambda h, i: (h, 0, i)),
                  pl.BlockSpec((s, HEAD_DIM), lambda h, i: (0, kcol + h)),
                  pl.BlockSpec((s, HEAD_DIM), lambda h, i: (0, vcol + h))],
        out_specs=pl.BlockSpec((tq, HEAD_DIM), lambda h, i: (i, h)),
        scratch_shapes=[pltpu.VMEM((HEAD_DIM + ONES_ROWS, s), jnp.bfloat16),
                        pltpu.VMEM((HEAD_DIM + ONES_ROWS, tq), jnp.float32),
                        pltpu.VMEM((ATTN_QB, MOBA_BLOCK, tq), jnp.float32),
                        pltpu.VMEM((2, MOBA_BLOCK, tq), jnp.bfloat16),
                        pltpu.VMEM((tq, tq), jnp.bfloat16)],
        compiler_params=_params(("arbitrary", "arbitrary"), 56),
        name="attention",
    )(proj, bias, proj, proj)


POOL_HALO = 16


def _pool_kernel(u_ref, halo_ref, w_ref, ls_ref, o_ref):
    i = pl.program_id(0)
    tm = u_ref.shape[0]
    u = u_ref[...].astype(jnp.float32)
    halo = jnp.where(i == 0, 0.0, halo_ref[...].astype(jnp.float32))
    t = i * tm + lax.broadcasted_iota(jnp.int32, (tm, 1), 0)
    for g, win in enumerate(POOL_WINDOWS):
        cols = slice(g * POOL_GW, (g + 1) * POOL_GW)
        ug = u[:, cols]
        e = jnp.concatenate([halo[:, cols], ug], axis=0)
        span = 1
        while span < win:
            e = e + pltpu.roll(e, span, axis=0)
            span *= 2
        cnt = jnp.minimum(t + 1, win).astype(jnp.float32)
        dg = e[POOL_HALO:, :] / cnt - ug
        mixed = _dot(dg.astype(jnp.bfloat16), w_ref[g])
        o_ref[:, cols] = (mixed * ls_ref[:, cols]).astype(o_ref.dtype)


def _pool(proj, w_grp, ls):
    s = proj.shape[0]
    tm = 1024
    ucol = U_COL // POOL_W
    return pl.pallas_call(
        _pool_kernel,
        out_shape=jax.ShapeDtypeStruct((s, POOL_W), jnp.bfloat16),
        grid=(s // tm,),
        in_specs=[pl.BlockSpec((tm, POOL_W), lambda i: (i, ucol)),
                  pl.BlockSpec((POOL_HALO, POOL_W),
                               lambda i: (jnp.maximum(i * (tm // POOL_HALO) - 1, 0), ucol)),
                  pl.BlockSpec(w_grp.shape, lambda i: (0, 0, 0)),
                  pl.BlockSpec((1, POOL_W), lambda i: (0, 0))],
        out_specs=pl.BlockSpec((tm, POOL_W), lambda i: (i, 0)),
        compiler_params=_params(("arbitrary",), 48),
        name="pool",
    )(proj, proj, w_grp, ls)


def _merge_kernel(a_ref, p_ref, ga_ref, gp_ref, x_ref, g1_ref, wa_ref, wp_ref, wo_ref, o_ref):
    j = pl.program_id(1)

    @pl.when(j == 0)
    def _():
        o_ref[...] = jnp.zeros_like(o_ref)

    ya = _dot(a_ref[...], wa_ref[...])
    yp = _dot(p_ref[...], wp_ref[...])
    merged = ga_ref[...].astype(jnp.float32) * ya + gp_ref[...].astype(jnp.float32) * yp
    o_ref[...] += _dot(merged.astype(jnp.bfloat16), wo_ref[...])

    @pl.when(j == pl.num_programs(1) - 1)
    def _():
        o_ref[...] = x_ref[...] + g1_ref[...] * o_ref[...]


def _merge(attn, pooled, proj, x2, g1, wa, wp, wo):
    s, d = x2.shape
    tm, tn = 512, 1024
    ga0, gp0 = G_COL // tn, (G_COL + d) // tn
    return pl.pallas_call(
        _merge_kernel,
        out_shape=jax.ShapeDtypeStruct((s, d), jnp.float32),
        grid=(s // tm, d // tn),
        in_specs=[pl.BlockSpec((tm, ATTN_W), lambda i, j: (i, 0)),
                  pl.BlockSpec((tm, POOL_W), lambda i, j: (i, 0)),
                  pl.BlockSpec((tm, tn), lambda i, j: (i, ga0 + j)),
                  pl.BlockSpec((tm, tn), lambda i, j: (i, gp0 + j)),
                  pl.BlockSpec((tm, d), lambda i, j: (i, 0)),
                  pl.BlockSpec((1, d), lambda i, j: (0, 0)),
                  pl.BlockSpec((ATTN_W, tn), lambda i, j: (0, j)),
                  pl.BlockSpec((POOL_W, tn), lambda i, j: (0, j)),
                  pl.BlockSpec((tn, d), lambda i, j: (j, 0))],
        out_specs=pl.BlockSpec((tm, d), lambda i, j: (i, 0)),
        compiler_params=_params(("arbitrary", "arbitrary"), 56),
        name="merge",
    )(attn, pooled, proj, proj, x2, g1, wa, wp, wo)


FFN_CHUNK = 256


def _ffn_kernel(x_ref, g_ref, sh_ref, sc_ref, g2_ref, wua_ref, wub_ref, cw_ref, cb_ref, wdp_ref, wd_ref,
                o_ref, h_scr, tail_scr, upa_scr, upb_scr):
    i = pl.program_id(0)
    j = pl.program_id(1)
    tm = x_ref.shape[0]
    n_chunks = cw_ref.shape[0] // 2

    @pl.when(j == 0)
    def _():
        _norm_modulate_into(h_scr, x_ref, g_ref, sh_ref, sc_ref)
        o_ref[...] = jnp.zeros_like(o_ref)

    @pl.when(i == 0)
    def _():
        tail_scr[j] = jnp.zeros(tail_scr.shape[1:], tail_scr.dtype)

    chunks = [slice(c * FFN_CHUNK, (c + 1) * FFN_CHUNK) for c in range(2)]

    def up_stage(c):
        for half, (w_ref, u_scr) in enumerate(((wua_ref, upa_scr), (wub_ref, upb_scr))):
            u_scr[c, :SUBLANES, :] = tail_scr[j, half, :, chunks[c]]
            up = _dot(h_scr[...], w_ref[:, chunks[c]])
            u_scr[c, SUBLANES:, :] = up
            tail_scr[j, half, :, chunks[c]] = up[tm - SUBLANES:, :]

    def conv(ext, n):
        cw = cw_ref[n]
        out = cb_ref[n] + cw[CONV_W - 1:CONV_W, :] * ext[SUBLANES:, :]
        for back in range(1, CONV_W):
            tap = CONV_W - 1 - back
            out = out + cw[tap:tap + 1, :] * pltpu.roll(ext, back, axis=0)[SUBLANES:, :]
        return out

    def gated(c, n):
        a = conv(upa_scr[c], n)
        b = conv(upb_scr[c], n_chunks + n)
        return (a * _sigmoid(a) * b).astype(jnp.bfloat16)

    up_stage(0)
    act_prev = gated(1, jnp.maximum(2 * j - 1, 0))
    act_prev = jnp.where(j > 0, act_prev, jnp.zeros_like(act_prev))
    up_stage(1)
    act_0 = gated(0, 2 * j)
    o_ref[...] += _dot(act_prev, wdp_ref[chunks[1], :]) + _dot(act_0, wd_ref[chunks[0], :])

    @pl.when(j == pl.num_programs(1) - 1)
    def _():
        o_ref[...] += _dot(gated(1, 2 * j + 1), wd_ref[chunks[1], :])
        o_ref[...] = x_ref[...] + g2_ref[...] * o_ref[...]


def _ffn(x1, g, sh, sc, g2, w_up, conv_w, conv_b, w_down):
    s, d = x1.shape
    tm, tf = 512, 2 * FFN_CHUNK
    nf = D_FF // tf
    n_chunks = 2 * D_FF // FFN_CHUNK
    cw = conv_w.reshape(CONV_W, n_chunks, FFN_CHUNK).transpose(1, 0, 2)
    cb = conv_b.reshape(n_chunks, 1, FFN_CHUNK)
    vec = lambda: pl.BlockSpec((1, d), lambda i, j: (0, 0))
    return pl.pallas_call(
        _ffn_kernel,
        out_shape=jax.ShapeDtypeStruct((s, d), jnp.float32),
        grid=(s // tm, nf),
        in_specs=[pl.BlockSpec((tm, d), lambda i, j: (i, 0)),
                  vec(), vec(), vec(), vec(),
                  pl.BlockSpec((d, tf), lambda i, j: (0, j)),
                  pl.BlockSpec((d, tf), lambda i, j: (0, nf + j)),
                  pl.BlockSpec(cw.shape, lambda i, j: (0, 0, 0)),
                  pl.BlockSpec(cb.shape, lambda i, j: (0, 0, 0)),
                  pl.BlockSpec((tf, d), lambda i, j: (jnp.maximum(j - 1, 0), 0)),
                  pl.BlockSpec((tf, d), lambda i, j: (j, 0))],
        out_specs=pl.BlockSpec((tm, d), lambda i, j: (i, 0)),
        scratch_shapes=[pltpu.VMEM((tm, d), jnp.bfloat16),
                        pltpu.VMEM((nf, 2, SUBLANES, tf), jnp.float32),
                        pltpu.VMEM((2, SUBLANES + tm, FFN_CHUNK), jnp.float32),
                        pltpu.VMEM((2, SUBLANES + tm, FFN_CHUNK), jnp.float32)],
        compiler_params=_params(("arbitrary", "arbitrary"), 48),
        name="ffn",
    )(x1, g, sh, sc, g2, w_up, w_up, cw, cb, w_down, w_down)


ROPE_LO = 128


def _rope_tables(s):
    half = HEAD_DIM // 2
    inv = ROPE_THETA ** (-jnp.arange(half, dtype=jnp.float32) / half)
    lo = jnp.arange(ROPE_LO).astype(jnp.float32)[:, None] * inv[None, :]
    hi = (jnp.arange(s // ROPE_LO) * ROPE_LO).astype(jnp.float32)[:, None] * inv[None, :]
    cl, sl = jnp.cos(lo)[None], jnp.sin(lo)[None]
    ch, sh = jnp.cos(hi)[:, None], jnp.sin(hi)[:, None]
    cos = (ch * cl - sh * sl).reshape(s, half)
    sin = (sh * cl + ch * sl).reshape(s, half)
    return jnp.concatenate([cos, cos], axis=1), jnp.concatenate([-sin, sin], axis=1)


def kernel(x, c, w_ada, b_ada, norm_mix_g, w_in, q_norm_g, k_norm_g, w_pool_grp, pool_scale, w_attn_br,
           w_pool_br, w_gate, b_gate, w_o, norm_ffn_g, w_up, conv_w, conv_b, w_down):
    batch, s, d = x.shape
    assert batch == 1 and d == D_MODEL and w_ada.shape[0] == 1
    bf = jnp.bfloat16
    row = lambda v: v.reshape(1, -1)
    cos_t, sin_t = _rope_tables(s)

    mod = _ada(c.reshape(d, 1), w_ada[0], row(b_ada[0]))
    sh1, sc1, g1, sh2, sc2, g2 = [mod[:, n * d:(n + 1) * d] for n in range(6)]

    x2 = x[0]
    proj = _proj(x2, row(norm_mix_g[0]), sh1, sc1, w_in[0].astype(bf), w_gate[0].astype(bf), row(b_gate[0]),
                 cos_t, sin_t, row(q_norm_g[0]), row(k_norm_g[0]))
    bias, w_up_bf, w_down_bf = _select(proj, w_up, w_down)
    attn = _attention(proj, bias)
    pooled = _pool(proj, w_pool_grp[0].astype(bf), row(pool_scale[0]))
    x1 = _merge(attn, pooled, proj, x2, g1, w_attn_br[0].astype(bf), w_pool_br[0].astype(bf),
                w_o[0].astype(bf))
    out = _ffn(x1, row(norm_ffn_g[0]), sh2, sc2, g2, w_up_bf, conv_w[0], conv_b[0], w_down_bf)
    return out[None]
```

```python
import functools
import math

import jax
import jax.numpy as jnp
from jax import lax
from jax.experimental import pallas as pl
from jax.experimental.pallas import tpu as pltpu

D_MODEL = 2048
N_HEADS = 8
HEAD_DIM = 128
ATTN_W = N_HEADS * HEAD_DIM
MOBA_BLOCK = 256
MOBA_TOPK = 3
POOL_WINDOWS = (2, 4, 8, 16)
POOL_W = 1024
POOL_GW = POOL_W // len(POOL_WINDOWS)
D_FF = 5632
CONV_W = 3
ROPE_THETA = 10000.0
EPS = 1e-6
NEG = -1e30
BLOCK_SHIFT = MOBA_BLOCK.bit_length() - 1
LOG2E = math.log2(math.e)

LANES = 128
SUBLANES = 8
PROJ_W = 4 * ATTN_W + 2 * D_MODEL
Q_COL, K_COL, V_COL, U_COL, G_COL = 0, ATTN_W, 2 * ATTN_W, 3 * ATTN_W, 4 * ATTN_W

MIB = 1024 * 1024


def _dot(a, b):
    return jnp.dot(a, b, preferred_element_type=jnp.float32)


def _dot_nt(a, b):
    return lax.dot_general(a, b, (((1,), (1,)), ((), ())), preferred_element_type=jnp.float32)


def _sigmoid(x):
    return 0.5 * jnp.tanh(0.5 * x) + 0.5


def _params(semantics, vmem_mib):
    return pltpu.CompilerParams(dimension_semantics=semantics, vmem_limit_bytes=vmem_mib * MIB)


NORM_ROWS = 64


def _norm_modulate_into(h_scr, x_ref, g_ref, shift_ref, scale_ref):
    gain = g_ref[...] * (1.0 + scale_ref[...])
    shift = shift_ref[...]

    def body(r, carry):
        rows = pl.ds(pl.multiple_of(r * NORM_ROWS, NORM_ROWS), NORM_ROWS)
        x = x_ref[rows, :]
        inv = lax.rsqrt(jnp.mean(x * x, axis=-1, keepdims=True) + EPS)
        h_scr[rows, :] = (x * inv * gain + shift).astype(h_scr.dtype)
        return carry

    lax.fori_loop(0, x_ref.shape[0] // NORM_ROWS, body, 0, unroll=4)


def _ada_kernel(c_ref, w_ref, b_ref, o_ref):
    c = c_ref[...]
    s = c * jax.nn.sigmoid(c)
    o_ref[...] = jnp.sum(s * w_ref[...], axis=0, keepdims=True) + b_ref[...]


def _ada(c_col, w_ada, b_ada):
    d, n = w_ada.shape
    tn = 1024
    return pl.pallas_call(
        _ada_kernel,
        out_shape=jax.ShapeDtypeStruct((1, n), jnp.float32),
        grid=(n // tn,),
        in_specs=[pl.BlockSpec((d, 1), lambda j: (0, 0)),
                  pl.BlockSpec((d, tn), lambda j: (0, j)),
                  pl.BlockSpec((1, tn), lambda j: (0, j))],
        out_specs=pl.BlockSpec((1, tn), lambda j: (0, j)),
        compiler_params=_params(("arbitrary",), 40),
        name="ada",
    )(c_col, w_ada, b_ada)


def _proj_kernel(x_ref, g_ref, sh_ref, sc_ref, wi_ref, wg_ref, cb_ref, cos_ref, sin_ref, qg_ref, kg_ref,
                 o_ref, h_scr, acc_scr, *, tn):
    j = pl.program_id(1)

    @pl.when(j == 0)
    def _():
        _norm_modulate_into(h_scr, x_ref, g_ref, sh_ref, sc_ref)

    @pl.when(j < G_COL // tn)
    def _():
        acc_scr[...] = _dot(h_scr[...], wi_ref[...])

    @pl.when(j >= G_COL // tn)
    def _():
        acc_scr[...] = _dot(h_scr[...], wg_ref[...])

    def head_norm_rope(gain_ref, out_scale):
        for hh in range(tn // HEAD_DIM):
            t = acc_scr[:, hh * HEAD_DIM:(hh + 1) * HEAD_DIM]
            ms = jnp.mean(t * t, axis=-1, keepdims=True)
            y = t * lax.rsqrt(ms + EPS) * gain_ref[...]
            r = y * cos_ref[...] + pltpu.roll(y, HEAD_DIM // 2, axis=1) * sin_ref[...]
            if out_scale != 1.0:
                r = r * out_scale
            o_ref[:, hh * HEAD_DIM:(hh + 1) * HEAD_DIM] = r.astype(o_ref.dtype)

    @pl.when(j < K_COL // tn)
    def _():
        head_norm_rope(qg_ref, HEAD_DIM ** -0.5 * LOG2E)

    @pl.when(jnp.logical_and(j >= K_COL // tn, j < V_COL // tn))
    def _():
        head_norm_rope(kg_ref, 1.0)

    @pl.when(jnp.logical_and(j >= V_COL // tn, j < G_COL // tn))
    def _():
        o_ref[...] = acc_scr[...].astype(o_ref.dtype)

    @pl.when(j >= G_COL // tn)
    def _():
        o_ref[...] = _sigmoid(acc_scr[...] + cb_ref[...]).astype(o_ref.dtype)


def _proj(x2, g, sh, sc, w_in, w_gate, b_gate, cos_t, sin_t, qg, kg):
    s, d = x2.shape
    tm, tn = 1024, 1024
    n_in = G_COL // tn
    vec = lambda: pl.BlockSpec((1, d), lambda i, j: (0, 0))
    return pl.pallas_call(
        functools.partial(_proj_kernel, tn=tn),
        out_shape=jax.ShapeDtypeStruct((s, PROJ_W), jnp.bfloat16),
        grid=(s // tm, PROJ_W // tn),
        in_specs=[pl.BlockSpec((tm, d), lambda i, j: (i, 0)),
                  vec(), vec(), vec(),
                  pl.BlockSpec((d, tn), lambda i, j: (0, jnp.minimum(j, n_in - 1))),
                  pl.BlockSpec((d, tn), lambda i, j: (0, jnp.maximum(j - n_in, 0))),
                  pl.BlockSpec((1, tn), lambda i, j: (0, jnp.maximum(j - n_in, 0))),
                  pl.BlockSpec((tm, HEAD_DIM), lambda i, j: (i, 0)),
                  pl.BlockSpec((tm, HEAD_DIM), lambda i, j: (i, 0)),
                  pl.BlockSpec((1, HEAD_DIM), lambda i, j: (0, 0)),
                  pl.BlockSpec((1, HEAD_DIM), lambda i, j: (0, 0))],
        out_specs=pl.BlockSpec((tm, tn), lambda i, j: (i, j)),
        scratch_shapes=[pltpu.VMEM((tm, d), jnp.bfloat16),
                        pltpu.VMEM((tm, tn), jnp.float32)],
        compiler_params=_params(("arbitrary", "arbitrary"), 56),
        name="proj",
    )(x2, g, sh, sc, w_in, w_gate, b_gate, cos_t, sin_t, qg, kg)


def _select_kernel(q_ref, k_ref, wu_ref, wd_ref, bias_ref, wu_out, wd_out, km_scr, *, tq):
    i = pl.program_id(1)
    nb = k_ref.shape[0] // MOBA_BLOCK
    wu_out[...] = wu_ref[...].astype(wu_out.dtype)
    wd_out[...] = wd_ref[...].astype(wd_out.dtype)

    @pl.when(i == 0)
    def _():
        k = k_ref[...].astype(jnp.float32).reshape(nb, MOBA_BLOCK, HEAD_DIM)
        km = jnp.sum(k, axis=1) * (1.0 / MOBA_BLOCK)
        p0 = km.astype(jnp.bfloat16)
        r1 = km - p0.astype(jnp.float32)
        p1 = r1.astype(jnp.bfloat16)
        p2 = (r1 - p1.astype(jnp.float32)).astype(jnp.bfloat16)
        km_scr[0] = p0
        km_scr[1] = p1
        km_scr[2] = p2

    q = q_ref[...]
    gate = _dot_nt(km_scr[0], q) + _dot_nt(km_scr[1], q) + _dot_nt(km_scr[2], q)
    blk = lax.broadcasted_iota(jnp.int32, gate.shape, 0)
    qpos = i * tq + lax.broadcasted_iota(jnp.int32, gate.shape, 1)
    own = lax.shift_right_logical(qpos, BLOCK_SHIFT)
    past = blk < own
    g = jnp.where(past, gate, NEG)
    bias = jnp.where(blk == own, 0.0, NEG)
    for _ in range(MOBA_TOPK):
        top = jnp.max(g, axis=0, keepdims=True)
        first = jnp.min(jnp.where(g == top, blk, nb), axis=0, keepdims=True)
        pick = blk == first
        bias = jnp.where(jnp.logical_and(pick, past), 0.0, bias)
        g = jnp.where(pick, -jnp.inf, g)
    bias_ref[...] = bias


def _eye(n):
    return (lax.broadcasted_iota(jnp.int32, (n, n), 0)
            == lax.broadcasted_iota(jnp.int32, (n, n), 1)).astype(jnp.bfloat16)


def _row_slab_spec(w, lead, *, n_outer, n_inner):
    rows, cols = w.shape[-2] // (n_outer * n_inner), w.shape[-1]
    if lead:
        return pl.BlockSpec((None, rows, cols), lambda h, i: (0, h * n_inner + i, 0))
    return pl.BlockSpec((rows, cols), lambda h, i: (h * n_inner + i, 0))


def _select(proj, w_up, w_down):
    s = proj.shape[0]
    tq = 4096
    kcol = K_COL // HEAD_DIM
    n_i = s // tq
    slab = functools.partial(_row_slab_spec, n_outer=N_HEADS, n_inner=n_i)
    return pl.pallas_call(
        functools.partial(_select_kernel, tq=tq),
        out_shape=[jax.ShapeDtypeStruct((N_HEADS, s // MOBA_BLOCK, s), jnp.float32),
                   jax.ShapeDtypeStruct(w_up.shape[1:], jnp.bfloat16),
                   jax.ShapeDtypeStruct(w_down.shape[1:], jnp.bfloat16)],
        grid=(N_HEADS, n_i),
        in_specs=[pl.BlockSpec((tq, HEAD_DIM), lambda h, i: (i, h)),
                  pl.BlockSpec((s, HEAD_DIM), lambda h, i: (0, kcol + h)),
                  slab(w_up, True), slab(w_down, True)],
        out_specs=[pl.BlockSpec((None, s // MOBA_BLOCK, tq), lambda h, i: (h, 0, i)),
                   slab(w_up, False), slab(w_down, False)],
        scratch_shapes=[pltpu.VMEM((3, s // MOBA_BLOCK, HEAD_DIM), jnp.bfloat16)],
        compiler_params=_params(("arbitrary", "arbitrary"), 48),
        name="select",
    )(proj, proj, w_up, w_down)


ATTN_QB = 8
ATTN_UNROLL = 8
ONES_ROWS = 16
XPOSE_CHUNK = 2048


def _attn_kernel(q_ref, bias_ref, k_ref, v_ref, wa_ref, wp_ref, wo_ref, o_ref, wa_out, wp_out, wo_out,
                 vaug_scr, acc_scr, s_scr, p_scr, ptail_scr):
    i = pl.program_id(1)
    for w_ref, w_out in ((wa_ref, wa_out), (wp_ref, wp_out), (wo_ref, wo_out)):
        w_out[...] = w_ref[...].astype(w_out.dtype)
    s_len = k_ref.shape[0]
    blk = MOBA_BLOCK

    @pl.when(i == 0)
    def _():
        for c in range(s_len // XPOSE_CHUNK):
            keys = slice(c * XPOSE_CHUNK, (c + 1) * XPOSE_CHUNK)
            vaug_scr[:HEAD_DIM, keys] = _dot_nt(_eye(HEAD_DIM), v_ref[keys, :]).astype(vaug_scr.dtype)
        vaug_scr[HEAD_DIM:, :] = jnp.ones((ONES_ROWS, s_len), vaug_scr.dtype)

    ng = ATTN_QB
    n_past = i * ng
    groups = [slice(g * blk, (g + 1) * blk) for g in range(ng)]
    key_le_query = (lax.broadcasted_iota(jnp.int32, (blk, blk), 0)
                    <= lax.broadcasted_iota(jnp.int32, (blk, blk), 1))

    def score_tile(j, slot, g, causal):
        kb = k_ref[pl.ds(pl.multiple_of(j * blk, blk), blk), :]
        s_t = _dot_nt(kb, q_ref[groups[g], :])
        if causal:
            s_t = jnp.where(key_le_query, s_t, NEG)
        s_scr[slot, :, groups[g]] = s_t
        return jnp.max(s_t, axis=0, keepdims=True)

    def softmax_tile(j, slot, g, m_old, mx):
        b = bias_ref[pl.ds(j, 1), groups[g]]
        m_new = jnp.maximum(m_old, mx + b)
        shift = jnp.maximum(m_new - b, mx)
        p_scr[slot, :, groups[g]] = jnp.exp2(s_scr[slot, :, groups[g]] - shift).astype(p_scr.dtype)
        return m_new, jnp.exp2(m_old - m_new)

    def value_tile(j, slot, g, alpha):
        vb = vaug_scr[:, pl.ds(pl.multiple_of(j * blk, blk), blk)]
        acc_scr[:, groups[g]] = alpha * acc_scr[:, groups[g]] + _dot(vb, p_scr[slot, :, groups[g]])

    def past_step(j, slot, m, mx, a_prev):
        mx_next, m_alpha = [], []
        for g in range(ng):
            value_tile(jnp.maximum(j - 1, 0), 1 - slot, g, a_prev[g])
            mx_next.append(score_tile(j + 1, 1 - slot, g, False))
            m_alpha.append(softmax_tile(j, slot, g, m[g], mx[g]))
        mx_next = tuple(mx_next)
        return tuple(ma[0] for ma in m_alpha), mx_next, tuple(ma[1] for ma in m_alpha)

    def body(jj, carry):
        for u in range(ATTN_UNROLL):
            carry = past_step(ATTN_UNROLL * jj + u, u % 2, *carry)
        return carry

    acc_scr[...] = jnp.zeros_like(acc_scr)
    p_scr[1] = jnp.zeros(p_scr.shape[1:], p_scr.dtype)
    init = (tuple(jnp.full((1, blk), NEG, jnp.float32) for _ in range(ng)),
            tuple(score_tile(0, 0, g, False) for g in range(ng)),
            tuple(jnp.ones((1, blk), jnp.float32) for _ in range(ng)))
    m, mx, alpha = lax.fori_loop(0, n_past // ATTN_UNROLL, body, init)
    m, mx, alpha = list(m), list(mx), list(alpha)

    for g in range(ng):
        value_tile(jnp.maximum(n_past - 1, 0), 1, g, alpha[g])
    mxs = {(g, 0): mx[g] for g in range(1, ng)}
    mxs[(0, 0)] = score_tile(n_past, 0, 0, True)
    for t in range(1, ng):
        for g in range(t, ng):
            mxs[(g, t)] = score_tile(n_past + t, t, g, g == t)
    span0 = pl.multiple_of(n_past * blk, blk)
    for g in range(ng):
        bs = [bias_ref[pl.ds(n_past + t, 1), groups[g]] for t in range(g + 1)]
        m_new = m[g]
        for t in range(g + 1):
            m_new = jnp.maximum(m_new, mxs[(g, t)] + bs[t])
        for t in range(g + 1):
            shift = jnp.maximum(m_new - bs[t], mxs[(g, t)])
            ptail_scr[t * blk:(t + 1) * blk, groups[g]] = (
                jnp.exp2(s_scr[t, :, groups[g]] - shift).astype(ptail_scr.dtype))
        pv = _dot(vaug_scr[:, pl.ds(span0, (g + 1) * blk)], ptail_scr[:(g + 1) * blk, groups[g]])
        acc_scr[:, groups[g]] = jnp.exp2(m[g] - m_new) * acc_scr[:, groups[g]] + pv
    o_ref[...] = (acc_scr[:HEAD_DIM, :] / acc_scr[HEAD_DIM:HEAD_DIM + 1, :]).T.astype(o_ref.dtype)


def _attention(proj, bias, w_attn_br, w_pool_br, w_o):
    s = proj.shape[0]
    tq = ATTN_QB * MOBA_BLOCK
    kcol, vcol = K_COL // HEAD_DIM, V_COL // HEAD_DIM
    weights = (w_attn_br, w_pool_br, w_o)
    slab = functools.partial(_row_slab_spec, n_outer=N_HEADS, n_inner=s // tq)
    return pl.pallas_call(
        _attn_kernel,
        out_shape=[jax.ShapeDtypeStruct((s, ATTN_W), jnp.bfloat16)]
                  + [jax.ShapeDtypeStruct(w.shape[1:], jnp.bfloat16) for w in weights],
        grid=(N_HEADS, s // tq),
        in_specs=[pl.BlockSpec((tq, HEAD_DIM), lambda h, i: (i, h)),
                  pl.BlockSpec((None, s // MOBA_BLOCK, tq), lambda h, i: (h, 0, i)),
                  pl.BlockSpec((s, HEAD_DIM), lambda h, i: (0, kcol + h)),
                  pl.BlockSpec((s, HEAD_DIM), lambda h, i: (0, vcol + h))]
                 + [slab(w, True) for w in weights],
        out_specs=[pl.BlockSpec((tq, HEAD_DIM), lambda h, i: (i, h))] + [slab(w, False) for w in weights],
        scratch_shapes=[pltpu.VMEM((HEAD_DIM + ONES_ROWS, s), jnp.bfloat16),
                        pltpu.VMEM((HEAD_DIM + ONES_ROWS, tq), jnp.float32),
                        pltpu.VMEM((ATTN_QB, MOBA_BLOCK, tq), jnp.float32),
                        pltpu.VMEM((2, MOBA_BLOCK, tq), jnp.bfloat16),
                        pltpu.VMEM((tq, tq), jnp.bfloat16)],
        compiler_params=_params(("arbitrary", "arbitrary"), 56),
        name="attention",
    )(proj, bias, proj, proj, *weights)


POOL_HALO = 16


def _pool_kernel(u_ref, halo_ref, w_ref, ls_ref, o_ref):
    i = pl.program_id(0)
    tm = u_ref.shape[0]
    u = u_ref[...].astype(jnp.float32)
    halo = jnp.where(i == 0, 0.0, halo_ref[...].astype(jnp.float32))
    t = i * tm + lax.broadcasted_iota(jnp.int32, (tm, 1), 0)
    for g, win in enumerate(POOL_WINDOWS):
        cols = slice(g * POOL_GW, (g + 1) * POOL_GW)
        ug = u[:, cols]
        e = jnp.concatenate([halo[:, cols], ug], axis=0)
        span = 1
        while span < win:
            e = e + pltpu.roll(e, span, axis=0)
            span *= 2
        cnt = jnp.minimum(t + 1, win).astype(jnp.float32)
        dg = e[POOL_HALO:, :] / cnt - ug
        mixed = _dot(dg.astype(jnp.bfloat16), w_ref[g])
        o_ref[:, cols] = (mixed * ls_ref[:, cols]).astype(o_ref.dtype)


def _pool(proj, w_grp, ls):
    s = proj.shape[0]
    tm = 1024
    ucol = U_COL // POOL_W
    return pl.pallas_call(
        _pool_kernel,
        out_shape=jax.ShapeDtypeStruct((s, POOL_W), jnp.bfloat16),
        grid=(s // tm,),
        in_specs=[pl.BlockSpec((tm, POOL_W), lambda i: (i, ucol)),
                  pl.BlockSpec((POOL_HALO, POOL_W),
                               lambda i: (jnp.maximum(i * (tm // POOL_HALO) - 1, 0), ucol)),
                  pl.BlockSpec(w_grp.shape, lambda i: (0, 0, 0)),
                  pl.BlockSpec((1, POOL_W), lambda i: (0, 0))],
        out_specs=pl.BlockSpec((tm, POOL_W), lambda i: (i, 0)),
        compiler_params=_params(("arbitrary",), 48),
        name="pool",
    )(proj, proj, w_grp, ls)


def _merge_kernel(a_ref, p_ref, ga_ref, gp_ref, x_ref, g1_ref, wa_ref, wp_ref, wo_ref, o_ref):
    j = pl.program_id(1)

    @pl.when(j == 0)
    def _():
        o_ref[...] = jnp.zeros_like(o_ref)

    ya = _dot(a_ref[...], wa_ref[...])
    yp = _dot(p_ref[...], wp_ref[...])
    merged = ga_ref[...].astype(jnp.float32) * ya + gp_ref[...].astype(jnp.float32) * yp
    o_ref[...] += _dot(merged.astype(jnp.bfloat16), wo_ref[...])

    @pl.when(j == pl.num_programs(1) - 1)
    def _():
        o_ref[...] = x_ref[...] + g1_ref[...] * o_ref[...]


def _merge(attn, pooled, proj, x2, g1, wa, wp, wo):
    s, d = x2.shape
    tm, tn = 512, 1024
    ga0, gp0 = G_COL // tn, (G_COL + d) // tn
    return pl.pallas_call(
        _merge_kernel,
        out_shape=jax.ShapeDtypeStruct((s, d), jnp.float32),
        grid=(s // tm, d // tn),
        in_specs=[pl.BlockSpec((tm, ATTN_W), lambda i, j: (i, 0)),
                  pl.BlockSpec((tm, POOL_W), lambda i, j: (i, 0)),
                  pl.BlockSpec((tm, tn), lambda i, j: (i, ga0 + j)),
                  pl.BlockSpec((tm, tn), lambda i, j: (i, gp0 + j)),
                  pl.BlockSpec((tm, d), lambda i, j: (i, 0)),
                  pl.BlockSpec((1, d), lambda i, j: (0, 0)),
                  pl.BlockSpec((ATTN_W, tn), lambda i, j: (0, j)),
                  pl.BlockSpec((POOL_W, tn), lambda i, j: (0, j)),
                  pl.BlockSpec((tn, d), lambda i, j: (j, 0))],
        out_specs=pl.BlockSpec((tm, d), lambda i, j: (i, 0)),
        compiler_params=_params(("arbitrary", "arbitrary"), 56),
        name="merge",
    )(attn, pooled, proj, proj, x2, g1, wa, wp, wo)


FFN_CHUNK = 256


def _ffn_kernel(x_ref, g_ref, sh_ref, sc_ref, g2_ref, wua_ref, wub_ref, cw_ref, cb_ref, wdp_ref, wd_ref,
                o_ref, h_scr, tail_scr, upa_scr, upb_scr):
    i = pl.program_id(0)
    j = pl.program_id(1)
    tm = x_ref.shape[0]
    n_chunks = cw_ref.shape[0] // 2

    @pl.when(j == 0)
    def _():
        _norm_modulate_into(h_scr, x_ref, g_ref, sh_ref, sc_ref)
        o_ref[...] = jnp.zeros_like(o_ref)

    @pl.when(i == 0)
    def _():
        tail_scr[j] = jnp.zeros(tail_scr.shape[1:], tail_scr.dtype)

    chunks = [slice(c * FFN_CHUNK, (c + 1) * FFN_CHUNK) for c in range(2)]

    def up_stage(c):
        for half, (w_ref, u_scr) in enumerate(((wua_ref, upa_scr), (wub_ref, upb_scr))):
            u_scr[c, :SUBLANES, :] = tail_scr[j, half, :, chunks[c]]
            up = _dot(h_scr[...], w_ref[:, chunks[c]])
            u_scr[c, SUBLANES:, :] = up
            tail_scr[j, half, :, chunks[c]] = up[tm - SUBLANES:, :]

    def conv(ext, n):
        cw = cw_ref[n]
        out = cb_ref[n] + cw[CONV_W - 1:CONV_W, :] * ext[SUBLANES:, :]
        for back in range(1, CONV_W):
            tap = CONV_W - 1 - back
            out = out + cw[tap:tap + 1, :] * pltpu.roll(ext, back, axis=0)[SUBLANES:, :]
        return out

    def gated(c, n):
        a = conv(upa_scr[c], n)
        b = conv(upb_scr[c], n_chunks + n)
        return (a * _sigmoid(a) * b).astype(jnp.bfloat16)

    up_stage(0)
    act_prev = gated(1, jnp.maximum(2 * j - 1, 0))
    act_prev = jnp.where(j > 0, act_prev, jnp.zeros_like(act_prev))
    up_stage(1)
    act_0 = gated(0, 2 * j)
    o_ref[...] += _dot(act_prev, wdp_ref[chunks[1], :]) + _dot(act_0, wd_ref[chunks[0], :])

    @pl.when(j == pl.num_programs(1) - 1)
    def _():
        o_ref[...] += _dot(gated(1, 2 * j + 1), wd_ref[chunks[1], :])
        o_ref[...] = x_ref[...] + g2_ref[...] * o_ref[...]


def _ffn(x1, g, sh, sc, g2, w_up, conv_w, conv_b, w_down):
    s, d = x1.shape
    tm, tf = 512, 2 * FFN_CHUNK
    nf = D_FF // tf
    n_chunks = 2 * D_FF // FFN_CHUNK
    cw = conv_w.reshape(CONV_W, n_chunks, FFN_CHUNK).transpose(1, 0, 2)
    cb = conv_b.reshape(n_chunks, 1, FFN_CHUNK)
    vec = lambda: pl.BlockSpec((1, d), lambda i, j: (0, 0))
    return pl.pallas_call(
        _ffn_kernel,
        out_shape=jax.ShapeDtypeStruct((s, d), jnp.float32),
        grid=(s // tm, nf),
        in_specs=[pl.BlockSpec((tm, d), lambda i, j: (i, 0)),
                  vec(), vec(), vec(), vec(),
                  pl.BlockSpec((d, tf), lambda i, j: (0, j)),
                  pl.BlockSpec((d, tf), lambda i, j: (0, nf + j)),
                  pl.BlockSpec(cw.shape, lambda i, j: (0, 0, 0)),
                  pl.BlockSpec(cb.shape, lambda i, j: (0, 0, 0)),
                  pl.BlockSpec((tf, d), lambda i, j: (jnp.maximum(j - 1, 0), 0)),
                  pl.BlockSpec((tf, d), lambda i, j: (j, 0))],
        out_specs=pl.BlockSpec((tm, d), lambda i, j: (i, 0)),
        scratch_shapes=[pltpu.VMEM((tm, d), jnp.bfloat16),
                        pltpu.VMEM((nf, 2, SUBLANES, tf), jnp.float32),
                        pltpu.VMEM((2, SUBLANES + tm, FFN_CHUNK), jnp.float32),
                        pltpu.VMEM((2, SUBLANES + tm, FFN_CHUNK), jnp.float32)],
        compiler_params=_params(("arbitrary", "arbitrary"), 48),
        name="ffn",
    )(x1, g, sh, sc, g2, w_up, w_up, cw, cb, w_down, w_down)


ROPE_LO = 128


def _rope_tables(s):
    half = HEAD_DIM // 2
    inv = ROPE_THETA ** (-jnp.arange(half, dtype=jnp.float32) / half)
    lo = jnp.arange(ROPE_LO).astype(jnp.float32)[:, None] * inv[None, :]
    hi = (jnp.arange(s // ROPE_LO) * ROPE_LO).astype(jnp.float32)[:, None] * inv[None, :]
    cl, sl = jnp.cos(lo)[None], jnp.sin(lo)[None]
    ch, sh = jnp.cos(hi)[:, None], jnp.sin(hi)[:, None]
    cos = (ch * cl - sh * sl).reshape(s, half)
    sin = (sh * cl + ch * sl).reshape(s, half)
    return jnp.concatenate([cos, cos], axis=1), jnp.concatenate([-sin, sin], axis=1)


def kernel(x, c, w_ada, b_ada, norm_mix_g, w_in, q_norm_g, k_norm_g, w_pool_grp, pool_scale, w_attn_br,
           w_pool_br, w_gate, b_gate, w_o, norm_ffn_g, w_up, conv_w, conv_b, w_down):
    batch, s, d = x.shape
    assert batch == 1 and d == D_MODEL and w_ada.shape[0] == 1
    bf = jnp.bfloat16
    row = lambda v: v.reshape(1, -1)
    cos_t, sin_t = _rope_tables(s)

    mod = _ada(c.reshape(d, 1), w_ada[0], row(b_ada[0]))
    sh1, sc1, g1, sh2, sc2, g2 = [mod[:, n * d:(n + 1) * d] for n in range(6)]

    x2 = x[0]
    proj = _proj(x2, row(norm_mix_g[0]), sh1, sc1, w_in[0].astype(bf), w_gate[0].astype(bf), row(b_gate[0]),
                 cos_t, sin_t, row(q_norm_g[0]), row(k_norm_g[0]))
    bias, w_up_bf, w_down_bf = _select(proj, w_up, w_down)
    attn, wa_bf, wp_bf, wo_bf = _attention(proj, bias, w_attn_br, w_pool_br, w_o)
    pooled = _pool(proj, w_pool_grp[0].astype(bf), row(pool_scale[0]))
    x1 = _merge(attn, pooled, proj, x2, g1, wa_bf, wp_bf, wo_bf)
    out = _ffn(x1, row(norm_ffn_g[0]), sh2, sc2, g2, w_up_bf, conv_w[0], conv_b[0], w_down_bf)
    return out[None]
```

```python
import functools
import math

import jax
import jax.numpy as jnp
from jax import lax
from jax.experimental import pallas as pl
from jax.experimental.pallas import tpu as pltpu

D_MODEL = 2048
N_HEADS = 8
HEAD_DIM = 128
ATTN_W = N_HEADS * HEAD_DIM
MOBA_BLOCK = 256
MOBA_TOPK = 3
POOL_WINDOWS = (2, 4, 8, 16)
POOL_W = 1024
POOL_GW = POOL_W // len(POOL_WINDOWS)
D_FF = 5632
CONV_W = 3
ROPE_THETA = 10000.0
EPS = 1e-6
NEG = -1e30
BLOCK_SHIFT = MOBA_BLOCK.bit_length() - 1
LOG2E = math.log2(math.e)

LANES = 128
SUBLANES = 8
PROJ_W = 4 * ATTN_W + 2 * D_MODEL
Q_COL, K_COL, V_COL, U_COL, G_COL = 0, ATTN_W, 2 * ATTN_W, 3 * ATTN_W, 4 * ATTN_W

MIB = 1024 * 1024


def _dot(a, b):
    return jnp.dot(a, b, preferred_element_type=jnp.float32)


def _dot_nt(a, b):
    return lax.dot_general(a, b, (((1,), (1,)), ((), ())), preferred_element_type=jnp.float32)


def _sigmoid(x):
    return 0.5 * jnp.tanh(0.5 * x) + 0.5


def _params(semantics, vmem_mib):
    return pltpu.CompilerParams(dimension_semantics=semantics, vmem_limit_bytes=vmem_mib * MIB)


NORM_ROWS = 64


def _norm_modulate_into(h_scr, x_ref, g_ref, shift_ref, scale_ref):
    gain = g_ref[...] * (1.0 + scale_ref[...])
    shift = shift_ref[...]

    def body(r, carry):
        rows = pl.ds(pl.multiple_of(r * NORM_ROWS, NORM_ROWS), NORM_ROWS)
        x = x_ref[rows, :]
        inv = lax.rsqrt(jnp.mean(x * x, axis=-1, keepdims=True) + EPS)
        h_scr[rows, :] = (x * inv * gain + shift).astype(h_scr.dtype)
        return carry

    lax.fori_loop(0, x_ref.shape[0] // NORM_ROWS, body, 0, unroll=4)


def _ada_kernel(c_ref, w_ref, b_ref, o_ref):
    c = c_ref[...]
    s = c * jax.nn.sigmoid(c)
    o_ref[...] = jnp.sum(s * w_ref[...], axis=0, keepdims=True) + b_ref[...]


def _ada(c_col, w_ada, b_ada):
    d, n = w_ada.shape
    tn = 1024
    return pl.pallas_call(
        _ada_kernel,
        out_shape=jax.ShapeDtypeStruct((1, n), jnp.float32),
        grid=(n // tn,),
        in_specs=[pl.BlockSpec((d, 1), lambda j: (0, 0)),
                  pl.BlockSpec((d, tn), lambda j: (0, j)),
                  pl.BlockSpec((1, tn), lambda j: (0, j))],
        out_specs=pl.BlockSpec((1, tn), lambda j: (0, j)),
        compiler_params=_params(("arbitrary",), 40),
        name="ada",
    )(c_col, w_ada, b_ada)


def _proj_kernel(x_ref, g_ref, sh_ref, sc_ref, wi_ref, wg_ref, cb_ref, cos_ref, sin_ref, qg_ref, kg_ref,
                 o_ref, h_scr, acc_scr, *, tn):
    j = pl.program_id(1)

    @pl.when(j == 0)
    def _():
        _norm_modulate_into(h_scr, x_ref, g_ref, sh_ref, sc_ref)

    @pl.when(j < G_COL // tn)
    def _():
        acc_scr[...] = _dot(h_scr[...], wi_ref[...])

    @pl.when(j >= G_COL // tn)
    def _():
        acc_scr[...] = _dot(h_scr[...], wg_ref[...])

    def head_norm_rope(gain_ref, out_scale):
        for hh in range(tn // HEAD_DIM):
            t = acc_scr[:, hh * HEAD_DIM:(hh + 1) * HEAD_DIM]
            ms = jnp.mean(t * t, axis=-1, keepdims=True)
            y = t * lax.rsqrt(ms + EPS) * gain_ref[...]
            r = y * cos_ref[...] + pltpu.roll(y, HEAD_DIM // 2, axis=1) * sin_ref[...]
            if out_scale != 1.0:
                r = r * out_scale
            o_ref[:, hh * HEAD_DIM:(hh + 1) * HEAD_DIM] = r.astype(o_ref.dtype)

    @pl.when(j < K_COL // tn)
    def _():
        head_norm_rope(qg_ref, HEAD_DIM ** -0.5 * LOG2E)

    @pl.when(jnp.logical_and(j >= K_COL // tn, j < V_COL // tn))
    def _():
        head_norm_rope(kg_ref, 1.0)

    @pl.when(jnp.logical_and(j >= V_COL // tn, j < G_COL // tn))
    def _():
        o_ref[...] = acc_scr[...].astype(o_ref.dtype)

    @pl.when(j >= G_COL // tn)
    def _():
        o_ref[...] = _sigmoid(acc_scr[...] + cb_ref[...]).astype(o_ref.dtype)


def _proj(x2, g, sh, sc, w_in, w_gate, b_gate, cos_t, sin_t, qg, kg):
    s, d = x2.shape
    tm, tn = 1024, 1024
    n_in = G_COL // tn
    vec = lambda: pl.BlockSpec((1, d), lambda i, j: (0, 0))
    return pl.pallas_call(
        functools.partial(_proj_kernel, tn=tn),
        out_shape=jax.ShapeDtypeStruct((s, PROJ_W), jnp.bfloat16),
        grid=(s // tm, PROJ_W // tn),
        in_specs=[pl.BlockSpec((tm, d), lambda i, j: (i, 0)),
                  vec(), vec(), vec(),
                  pl.BlockSpec((d, tn), lambda i, j: (0, jnp.minimum(j, n_in - 1))),
                  pl.BlockSpec((d, tn), lambda i, j: (0, jnp.maximum(j - n_in, 0))),
                  pl.BlockSpec((1, tn), lambda i, j: (0, jnp.maximum(j - n_in, 0))),
                  pl.BlockSpec((tm, HEAD_DIM), lambda i, j: (i, 0)),
                  pl.BlockSpec((tm, HEAD_DIM), lambda i, j: (i, 0)),
                  pl.BlockSpec((1, HEAD_DIM), lambda i, j: (0, 0)),
                  pl.BlockSpec((1, HEAD_DIM), lambda i, j: (0, 0))],
        out_specs=pl.BlockSpec((tm, tn), lambda i, j: (i, j)),
        scratch_shapes=[pltpu.VMEM((tm, d), jnp.bfloat16),
                        pltpu.VMEM((tm, tn), jnp.float32)],
        compiler_params=_params(("arbitrary", "arbitrary"), 56),
        name="proj",
    )(x2, g, sh, sc, w_in, w_gate, b_gate, cos_t, sin_t, qg, kg)


def _select_kernel(q_ref, k_ref, wu_ref, wd_ref, bias_ref, wu_out, wd_out, km_scr, *, tq):
    i = pl.program_id(1)
    nb = k_ref.shape[0] // MOBA_BLOCK
    wu_out[...] = wu_ref[...].astype(wu_out.dtype)
    wd_out[...] = wd_ref[...].astype(wd_out.dtype)

    @pl.when(i == 0)
    def _():
        k = k_ref[...].astype(jnp.float32).reshape(nb, MOBA_BLOCK, HEAD_DIM)
        km = jnp.sum(k, axis=1) * (1.0 / MOBA_BLOCK)
        p0 = km.astype(jnp.bfloat16)
        r1 = km - p0.astype(jnp.float32)
        p1 = r1.astype(jnp.bfloat16)
        p2 = (r1 - p1.astype(jnp.float32)).astype(jnp.bfloat16)
        km_scr[0] = p0
        km_scr[1] = p1
        km_scr[2] = p2

    q = q_ref[...]
    gate = _dot_nt(km_scr[0], q) + _dot_nt(km_scr[1], q) + _dot_nt(km_scr[2], q)
    blk = lax.broadcasted_iota(jnp.int32, gate.shape, 0)
    qpos = i * tq + lax.broadcasted_iota(jnp.int32, gate.shape, 1)
    own = lax.shift_right_logical(qpos, BLOCK_SHIFT)
    past = blk < own
    g = jnp.where(past, gate, NEG)
    bias = jnp.where(blk == own, 0.0, NEG)
    for _ in range(MOBA_TOPK):
        top = jnp.max(g, axis=0, keepdims=True)
        first = jnp.min(jnp.where(g == top, blk, nb), axis=0, keepdims=True)
        pick = blk == first
        bias = jnp.where(jnp.logical_and(pick, past), 0.0, bias)
        g = jnp.where(pick, -jnp.inf, g)
    bias_ref[...] = bias


def _eye(n):
    return (lax.broadcasted_iota(jnp.int32, (n, n), 0)
            == lax.broadcasted_iota(jnp.int32, (n, n), 1)).astype(jnp.bfloat16)


def _row_slab_spec(w, lead, *, n_outer, n_inner):
    rows, cols = w.shape[-2] // (n_outer * n_inner), w.shape[-1]
    if lead:
        return pl.BlockSpec((None, rows, cols), lambda h, i: (0, h * n_inner + i, 0))
    return pl.BlockSpec((rows, cols), lambda h, i: (h * n_inner + i, 0))


def _select(proj, w_up, w_down):
    s = proj.shape[0]
    tq = 4096
    kcol = K_COL // HEAD_DIM
    n_i = s // tq
    slab = functools.partial(_row_slab_spec, n_outer=N_HEADS, n_inner=n_i)
    return pl.pallas_call(
        functools.partial(_select_kernel, tq=tq),
        out_shape=[jax.ShapeDtypeStruct((N_HEADS, s // MOBA_BLOCK, s), jnp.float32),
                   jax.ShapeDtypeStruct(w_up.shape[1:], jnp.bfloat16),
                   jax.ShapeDtypeStruct(w_down.shape[1:], jnp.bfloat16)],
        grid=(N_HEADS, n_i),
        in_specs=[pl.BlockSpec((tq, HEAD_DIM), lambda h, i: (i, h)),
                  pl.BlockSpec((s, HEAD_DIM), lambda h, i: (0, kcol + h)),
                  slab(w_up, True), slab(w_down, True)],
        out_specs=[pl.BlockSpec((None, s // MOBA_BLOCK, tq), lambda h, i: (h, 0, i)),
                   slab(w_up, False), slab(w_down, False)],
        scratch_shapes=[pltpu.VMEM((3, s // MOBA_BLOCK, HEAD_DIM), jnp.bfloat16)],
        compiler_params=_params(("arbitrary", "arbitrary"), 48),
        name="select",
    )(proj, proj, w_up, w_down)


ATTN_QB = 8
ATTN_UNROLL = 8
ONES_ROWS = 16
XPOSE_CHUNK = 2048


def _attn_kernel(q_ref, bias_ref, k_ref, v_ref, wa_ref, wp_ref, wo_ref, o_ref, wa_out, wp_out, wo_out,
                 vaug_scr, acc_scr, s_scr, p_scr, ptail_scr):
    i = pl.program_id(1)
    for w_ref, w_out in ((wa_ref, wa_out), (wp_ref, wp_out), (wo_ref, wo_out)):
        w_out[...] = w_ref[...].astype(w_out.dtype)
    s_len = k_ref.shape[0]
    blk = MOBA_BLOCK

    @pl.when(i == 0)
    def _():
        for c in range(s_len // XPOSE_CHUNK):
            keys = slice(c * XPOSE_CHUNK, (c + 1) * XPOSE_CHUNK)
            vaug_scr[:HEAD_DIM, keys] = _dot_nt(_eye(HEAD_DIM), v_ref[keys, :]).astype(vaug_scr.dtype)
        vaug_scr[HEAD_DIM:, :] = jnp.ones((ONES_ROWS, s_len), vaug_scr.dtype)

    ng = ATTN_QB
    n_past = i * ng
    groups = [slice(g * blk, (g + 1) * blk) for g in range(ng)]
    key_le_query = (lax.broadcasted_iota(jnp.int32, (blk, blk), 0)
                    <= lax.broadcasted_iota(jnp.int32, (blk, blk), 1))

    def score_tile(j, slot, g, causal):
        kb = k_ref[pl.ds(pl.multiple_of(j * blk, blk), blk), :]
        s_t = _dot_nt(kb, q_ref[groups[g], :])
        if causal:
            s_t = jnp.where(key_le_query, s_t, NEG)
        s_scr[slot, :, groups[g]] = s_t
        return jnp.max(s_t, axis=0, keepdims=True)

    def softmax_tile(j, slot, g, m_old, mx):
        b = bias_ref[pl.ds(j, 1), groups[g]]
        m_new = jnp.maximum(m_old, mx + b)
        shift = jnp.maximum(m_new - b, mx)
        p_scr[slot, :, groups[g]] = jnp.exp2(s_scr[slot, :, groups[g]] - shift).astype(p_scr.dtype)
        return m_new, jnp.exp2(m_old - m_new)

    def value_tile(j, slot, g, alpha):
        vb = vaug_scr[:, pl.ds(pl.multiple_of(j * blk, blk), blk)]
        acc_scr[:, groups[g]] = alpha * acc_scr[:, groups[g]] + _dot(vb, p_scr[slot, :, groups[g]])

    def past_step(j, slot, m, mx, a_prev):
        mx_next, m_alpha = [], []
        for g in range(ng):
            value_tile(jnp.maximum(j - 1, 0), 1 - slot, g, a_prev[g])
            mx_next.append(score_tile(j + 1, 1 - slot, g, False))
            m_alpha.append(softmax_tile(j, slot, g, m[g], mx[g]))
        mx_next = tuple(mx_next)
        return tuple(ma[0] for ma in m_alpha), mx_next, tuple(ma[1] for ma in m_alpha)

    def body(jj, carry):
        for u in range(ATTN_UNROLL):
            carry = past_step(ATTN_UNROLL * jj + u, u % 2, *carry)
        return carry

    acc_scr[...] = jnp.zeros_like(acc_scr)
    p_scr[1] = jnp.zeros(p_scr.shape[1:], p_scr.dtype)
    init = (tuple(jnp.full((1, blk), NEG, jnp.float32) for _ in range(ng)),
            tuple(score_tile(0, 0, g, False) for g in range(ng)),
            tuple(jnp.ones((1, blk), jnp.float32) for _ in range(ng)))
    m, mx, alpha = lax.fori_loop(0, n_past // ATTN_UNROLL, body, init)
    m, mx, alpha = list(m), list(mx), list(alpha)

    for g in range(ng):
        value_tile(jnp.maximum(n_past - 1, 0), 1, g, alpha[g])
    mxs = {(g, 0): mx[g] for g in range(1, ng)}
    mxs[(0, 0)] = score_tile(n_past, 0, 0, True)
    for t in range(1, ng):
        for g in range(t, ng):
            mxs[(g, t)] = score_tile(n_past + t, t, g, g == t)
    span0 = pl.multiple_of(n_past * blk, blk)
    for g in range(ng):
        bs = [bias_ref[pl.ds(n_past + t, 1), groups[g]] for t in range(g + 1)]
        m_new = m[g]
        for t in range(g + 1):
            m_new = jnp.maximum(m_new, mxs[(g, t)] + bs[t])
        for t in range(g + 1):
            shift = jnp.maximum(m_new - bs[t], mxs[(g, t)])
            ptail_scr[t * blk:(t + 1) * blk, groups[g]] = (
                jnp.exp2(s_scr[t, :, groups[g]] - shift).astype(ptail_scr.dtype))
        pv = _dot(vaug_scr[:, pl.ds(span0, (g + 1) * blk)], ptail_scr[:(g + 1) * blk, groups[g]])
        acc_scr[:, groups[g]] = jnp.exp2(m[g] - m_new) * acc_scr[:, groups[g]] + pv
    o_ref[...] = (acc_scr[:HEAD_DIM, :] / acc_scr[HEAD_DIM:HEAD_DIM + 1, :]).T.astype(o_ref.dtype)


def _attention(proj, bias, w_attn_br, w_pool_br, w_o):
    s = proj.shape[0]
    tq = ATTN_QB * MOBA_BLOCK
    kcol, vcol = K_COL // HEAD_DIM, V_COL // HEAD_DIM
    weights = (w_attn_br, w_pool_br, w_o)
    slab = functools.partial(_row_slab_spec, n_outer=N_HEADS, n_inner=s // tq)
    return pl.pallas_call(
        _attn_kernel,
        out_shape=[jax.ShapeDtypeStruct((s, ATTN_W), jnp.bfloat16)]
                  + [jax.ShapeDtypeStruct(w.shape[1:], jnp.bfloat16) for w in weights],
        grid=(N_HEADS, s // tq),
        in_specs=[pl.BlockSpec((tq, HEAD_DIM), lambda h, i: (i, h)),
                  pl.BlockSpec((None, s // MOBA_BLOCK, tq), lambda h, i: (h, 0, i)),
                  pl.BlockSpec((s, HEAD_DIM), lambda h, i: (0, kcol + h)),
                  pl.BlockSpec((s, HEAD_DIM), lambda h, i: (0, vcol + h))]
                 + [slab(w, True) for w in weights],
        out_specs=[pl.BlockSpec((tq, HEAD_DIM), lambda h, i: (i, h))] + [slab(w, False) for w in weights],
        scratch_shapes=[pltpu.VMEM((HEAD_DIM + ONES_ROWS, s), jnp.bfloat16),
                        pltpu.VMEM((HEAD_DIM + ONES_ROWS, tq), jnp.float32),
                        pltpu.VMEM((ATTN_QB, MOBA_BLOCK, tq), jnp.float32),
                        pltpu.VMEM((2, MOBA_BLOCK, tq), jnp.bfloat16),
                        pltpu.VMEM((tq, tq), jnp.bfloat16)],
        compiler_params=_params(("arbitrary", "arbitrary"), 56),
        name="attention",
    )(proj, bias, proj, proj, *weights)


POOL_HALO = 16


def _pool_kernel(u_ref, halo_ref, w_ref, ls_ref, o_ref):
    i = pl.program_id(0)
    tm = u_ref.shape[0]
    u = u_ref[...].astype(jnp.float32)
    halo = jnp.where(i == 0, 0.0, halo_ref[...].astype(jnp.float32))
    t = i * tm + lax.broadcasted_iota(jnp.int32, (tm, 1), 0)
    for g, win in enumerate(POOL_WINDOWS):
        cols = slice(g * POOL_GW, (g + 1) * POOL_GW)
        ug = u[:, cols]
        e = jnp.concatenate([halo[:, cols], ug], axis=0)
        span = 1
        while span < win:
            e = e + pltpu.roll(e, span, axis=0)
            span *= 2
        cnt = jnp.minimum(t + 1, win).astype(jnp.float32)
        dg = e[POOL_HALO:, :] / cnt - ug
        mixed = _dot(dg.astype(jnp.bfloat16), w_ref[g])
        o_ref[:, cols] = (mixed * ls_ref[:, cols]).astype(o_ref.dtype)


def _pool(proj, w_grp, ls):
    s = proj.shape[0]
    tm = 1024
    ucol = U_COL // POOL_W
    return pl.pallas_call(
        _pool_kernel,
        out_shape=jax.ShapeDtypeStruct((s, POOL_W), jnp.bfloat16),
        grid=(s // tm,),
        in_specs=[pl.BlockSpec((tm, POOL_W), lambda i: (i, ucol)),
                  pl.BlockSpec((POOL_HALO, POOL_W),
                               lambda i: (jnp.maximum(i * (tm // POOL_HALO) - 1, 0), ucol)),
                  pl.BlockSpec(w_grp.shape, lambda i: (0, 0, 0)),
                  pl.BlockSpec((1, POOL_W), lambda i: (0, 0))],
        out_specs=pl.BlockSpec((tm, POOL_W), lambda i: (i, 0)),
        compiler_params=_params(("arbitrary",), 48),
        name="pool",
    )(proj, proj, w_grp, ls)


def _merge_kernel(a_ref, p_ref, ga_ref, gp_ref, x_ref, g1_ref, wa_ref, wp_ref, wo_ref, o_ref):
    j = pl.program_id(1)

    @pl.when(j == 0)
    def _():
        o_ref[...] = jnp.zeros_like(o_ref)

    ya = _dot(a_ref[...], wa_ref[...])
    yp = _dot(p_ref[...], wp_ref[...])
    merged = ga_ref[...].astype(jnp.float32) * ya + gp_ref[...].astype(jnp.float32) * yp
    o_ref[...] += _dot(merged.astype(jnp.bfloat16), wo_ref[...])

    @pl.when(j == pl.num_programs(1) - 1)
    def _():
        o_ref[...] = x_ref[...] + g1_ref[...] * o_ref[...]


def _merge(attn, pooled, proj, x2, g1, wa, wp, wo):
    s, d = x2.shape
    tm, tn = 512, 1024
    ga0, gp0 = G_COL // tn, (G_COL + d) // tn
    return pl.pallas_call(
        _merge_kernel,
        out_shape=jax.ShapeDtypeStruct((s, d), jnp.float32),
        grid=(s // tm, d // tn),
        in_specs=[pl.BlockSpec((tm, ATTN_W), lambda i, j: (i, 0)),
                  pl.BlockSpec((tm, POOL_W), lambda i, j: (i, 0)),
                  pl.BlockSpec((tm, tn), lambda i, j: (i, ga0 + j)),
                  pl.BlockSpec((tm, tn), lambda i, j: (i, gp0 + j)),
                  pl.BlockSpec((tm, d), lambda i, j: (i, 0)),
                  pl.BlockSpec((1, d), lambda i, j: (0, 0)),
                  pl.BlockSpec((ATTN_W, tn), lambda i, j: (0, j)),
                  pl.BlockSpec((POOL_W, tn), lambda i, j: (0, j)),
                  pl.BlockSpec((tn, d), lambda i, j: (j, 0))],
        out_specs=pl.BlockSpec((tm, d), lambda i, j: (i, 0)),
        compiler_params=_params(("arbitrary", "arbitrary"), 56),
        name="merge",
    )(attn, pooled, proj, proj, x2, g1, wa, wp, wo)


FFN_CHUNK = 256


def _ffn_kernel(x_ref, g_ref, sh_ref, sc_ref, g2_ref, wua_ref, wub_ref, cw_ref, cb_ref, wdp_ref, wd_ref,
                o_ref, h_scr, tail_scr, upa_scr, upb_scr):
    i = pl.program_id(0)
    j = pl.program_id(1)
    tm = x_ref.shape[0]
    n_chunks = cw_ref.shape[0] // 2

    @pl.when(j == 0)
    def _():
        _norm_modulate_into(h_scr, x_ref, g_ref, sh_ref, sc_ref)
        o_ref[...] = jnp.zeros_like(o_ref)
        upa_scr[1] = jnp.zeros(upa_scr.shape[1:], upa_scr.dtype)
        upb_scr[1] = jnp.zeros(upb_scr.shape[1:], upb_scr.dtype)

    @pl.when(i == 0)
    def _():
        tail_scr[j] = jnp.zeros(tail_scr.shape[1:], tail_scr.dtype)

    chunks = [slice(c * FFN_CHUNK, (c + 1) * FFN_CHUNK) for c in range(2)]

    def up_stage(c):
        for half, (w_ref, u_scr) in enumerate(((wua_ref, upa_scr), (wub_ref, upb_scr))):
            u_scr[c, :SUBLANES, :] = tail_scr[j, half, :, chunks[c]]
            up = _dot(h_scr[...], w_ref[:, chunks[c]])
            u_scr[c, SUBLANES:, :] = up
            tail_scr[j, half, :, chunks[c]] = up[tm - SUBLANES:, :]

    def conv(ext, n):
        cw = cw_ref[n]
        out = cb_ref[n] + cw[CONV_W - 1:CONV_W, :] * ext[SUBLANES:, :]
        for back in range(1, CONV_W):
            tap = CONV_W - 1 - back
            out = out + cw[tap:tap + 1, :] * pltpu.roll(ext, back, axis=0)[SUBLANES:, :]
        return out

    def gated(c, n):
        a = conv(upa_scr[c], n)
        b = conv(upb_scr[c], n_chunks + n)
        return (a * _sigmoid(a) * b).astype(jnp.bfloat16)

    up_stage(0)
    act_prev = gated(1, jnp.maximum(2 * j - 1, 0))
    act_prev = jnp.where(j > 0, act_prev, jnp.zeros_like(act_prev))
    up_stage(1)
    act_0 = gated(0, 2 * j)
    o_ref[...] += _dot(act_prev, wdp_ref[chunks[1], :]) + _dot(act_0, wd_ref[chunks[0], :])

    @pl.when(j == pl.num_programs(1) - 1)
    def _():
        o_ref[...] += _dot(gated(1, 2 * j + 1), wd_ref[chunks[1], :])
        o_ref[...] = x_ref[...] + g2_ref[...] * o_ref[...]


def _ffn(x1, g, sh, sc, g2, w_up, conv_w, conv_b, w_down):
    s, d = x1.shape
    tm, tf = 512, 2 * FFN_CHUNK
    nf = D_FF // tf
    n_chunks = 2 * D_FF // FFN_CHUNK
    cw = conv_w.reshape(CONV_W, n_chunks, FFN_CHUNK).transpose(1, 0, 2)
    cb = conv_b.reshape(n_chunks, 1, FFN_CHUNK)
    vec = lambda: pl.BlockSpec((1, d), lambda i, j: (0, 0))
    return pl.pallas_call(
        _ffn_kernel,
        out_shape=jax.ShapeDtypeStruct((s, d), jnp.float32),
        grid=(s // tm, nf),
        in_specs=[pl.BlockSpec((tm, d), lambda i, j: (i, 0)),
                  vec(), vec(), vec(), vec(),
                  pl.BlockSpec((d, tf), lambda i, j: (0, j)),
                  pl.BlockSpec((d, tf), lambda i, j: (0, nf + j)),
                  pl.BlockSpec(cw.shape, lambda i, j: (0, 0, 0)),
                  pl.BlockSpec(cb.shape, lambda i, j: (0, 0, 0)),
                  pl.BlockSpec((tf, d), lambda i, j: (jnp.maximum(j - 1, 0), 0)),
                  pl.BlockSpec((tf, d), lambda i, j: (j, 0))],
        out_specs=pl.BlockSpec((tm, d), lambda i, j: (i, 0)),
        scratch_shapes=[pltpu.VMEM((tm, d), jnp.bfloat16),
                        pltpu.VMEM((nf, 2, SUBLANES, tf), jnp.float32),
                        pltpu.VMEM((2, SUBLANES + tm, FFN_CHUNK), jnp.float32),
                        pltpu.VMEM((2, SUBLANES + tm, FFN_CHUNK), jnp.float32)],
        compiler_params=_params(("arbitrary", "arbitrary"), 48),
        name="ffn",
    )(x1, g, sh, sc, g2, w_up, w_up, cw, cb, w_down, w_down)


ROPE_LO = 128


def _rope_tables(s):
    half = HEAD_DIM // 2
    inv = ROPE_THETA ** (-jnp.arange(half, dtype=jnp.float32) / half)
    lo = jnp.arange(ROPE_LO).astype(jnp.float32)[:, None] * inv[None, :]
    hi = (jnp.arange(s // ROPE_LO) * ROPE_LO).astype(jnp.float32)[:, None] * inv[None, :]
    cl, sl = jnp.cos(lo)[None], jnp.sin(lo)[None]
    ch, sh = jnp.cos(hi)[:, None], jnp.sin(hi)[:, None]
    cos = (ch * cl - sh * sl).reshape(s, half)
    sin = (sh * cl + ch * sl).reshape(s, half)
    return jnp.concatenate([cos, cos], axis=1), jnp.concatenate([-sin, sin], axis=1)


def kernel(x, c, w_ada, b_ada, norm_mix_g, w_in, q_norm_g, k_norm_g, w_pool_grp, pool_scale, w_attn_br,
           w_pool_br, w_gate, b_gate, w_o, norm_ffn_g, w_up, conv_w, conv_b, w_down):
    batch, s, d = x.shape
    assert batch == 1 and d == D_MODEL and w_ada.shape[0] == 1
    bf = jnp.bfloat16
    row = lambda v: v.reshape(1, -1)
    cos_t, sin_t = _rope_tables(s)

    mod = _ada(c.reshape(d, 1), w_ada[0], row(b_ada[0]))
    sh1, sc1, g1, sh2, sc2, g2 = [mod[:, n * d:(n + 1) * d] for n in range(6)]

    x2 = x[0]
    proj = _proj(x2, row(norm_mix_g[0]), sh1, sc1, w_in[0].astype(bf), w_gate[0].astype(bf), row(b_gate[0]),
                 cos_t, sin_t, row(q_norm_g[0]), row(k_norm_g[0]))
    bias, w_up_bf, w_down_bf = _select(proj, w_up, w_down)
    attn, wa_bf, wp_bf, wo_bf = _attention(proj, bias, w_attn_br, w_pool_br, w_o)
    pooled = _pool(proj, w_pool_grp[0].astype(bf), row(pool_scale[0]))
    x1 = _merge(attn, pooled, proj, x2, g1, wa_bf, wp_bf, wo_bf)
    out = _ffn(x1, row(norm_ffn_g[0]), sh2, sc2, g2, w_up_bf, conv_w[0], conv_b[0], w_down_bf)
    return out[None]
```

```python
import functools
import math

import jax
import jax.numpy as jnp
from jax import lax
from jax.experimental import pallas as pl
from jax.experimental.pallas import tpu as pltpu

D_MODEL = 2048
N_HEADS = 8
HEAD_DIM = 128
ATTN_W = N_HEADS * HEAD_DIM
MOBA_BLOCK = 256
MOBA_TOPK = 3
POOL_WINDOWS = (2, 4, 8, 16)
POOL_W = 1024
POOL_GW = POOL_W // len(POOL_WINDOWS)
D_FF = 5632
CONV_W = 3
ROPE_THETA = 10000.0
EPS = 1e-6
NEG = -1e30
BLOCK_SHIFT = MOBA_BLOCK.bit_length() - 1
LOG2E = math.log2(math.e)

LANES = 128
SUBLANES = 8
PROJ_W = 4 * ATTN_W + 2 * D_MODEL
Q_COL, K_COL, V_COL, U_COL, G_COL = 0, ATTN_W, 2 * ATTN_W, 3 * ATTN_W, 4 * ATTN_W

MIB = 1024 * 1024


def _dot(a, b):
    return jnp.dot(a, b, preferred_element_type=jnp.float32)


def _dot_nt(a, b):
    return lax.dot_general(a, b, (((1,), (1,)), ((), ())), preferred_element_type=jnp.float32)


def _sigmoid(x):
    return 0.5 * jnp.tanh(0.5 * x) + 0.5


def _params(semantics, vmem_mib):
    return pltpu.CompilerParams(dimension_semantics=semantics, vmem_limit_bytes=vmem_mib * MIB)


NORM_ROWS = 64


def _norm_modulate_into(h_scr, x_ref, g_ref, shift_ref, scale_ref):
    gain = g_ref[...] * (1.0 + scale_ref[...])
    shift = shift_ref[...]

    def body(r, carry):
        rows = pl.ds(pl.multiple_of(r * NORM_ROWS, NORM_ROWS), NORM_ROWS)
        x = x_ref[rows, :]
        inv = lax.rsqrt(jnp.mean(x * x, axis=-1, keepdims=True) + EPS)
        h_scr[rows, :] = (x * inv * gain + shift).astype(h_scr.dtype)
        return carry

    lax.fori_loop(0, x_ref.shape[0] // NORM_ROWS, body, 0, unroll=4)


def _ada_kernel(c_ref, w_ref, b_ref, o_ref):
    c = c_ref[...]
    s = c * jax.nn.sigmoid(c)
    o_ref[...] = jnp.sum(s * w_ref[...], axis=0, keepdims=True) + b_ref[...]


def _ada(c_col, w_ada, b_ada):
    d, n = w_ada.shape
    tn = 1024
    return pl.pallas_call(
        _ada_kernel,
        out_shape=jax.ShapeDtypeStruct((1, n), jnp.float32),
        grid=(n // tn,),
        in_specs=[pl.BlockSpec((d, 1), lambda j: (0, 0)),
                  pl.BlockSpec((d, tn), lambda j: (0, j)),
                  pl.BlockSpec((1, tn), lambda j: (0, j))],
        out_specs=pl.BlockSpec((1, tn), lambda j: (0, j)),
        compiler_params=_params(("arbitrary",), 40),
        name="ada",
    )(c_col, w_ada, b_ada)


def _proj_kernel(x_ref, g_ref, sh_ref, sc_ref, wi_ref, wg_ref, cb_ref, cos_ref, sin_ref, qg_ref, kg_ref,
                 o_ref, h_scr, acc_scr, *, tn):
    j = pl.program_id(1)

    @pl.when(j == 0)
    def _():
        _norm_modulate_into(h_scr, x_ref, g_ref, sh_ref, sc_ref)

    @pl.when(j < G_COL // tn)
    def _():
        acc_scr[...] = _dot(h_scr[...], wi_ref[...])

    @pl.when(j >= G_COL // tn)
    def _():
        halves = [slice(c * (tn // 2), (c + 1) * (tn // 2)) for c in range(2)]
        for cols in halves:
            acc_scr[:, cols] = _dot(h_scr[...], wg_ref[:, cols])
        for cols in halves:
            o_ref[:, cols] = _sigmoid(acc_scr[:, cols] + cb_ref[:, cols]).astype(o_ref.dtype)

    def head_norm_rope(gain_ref, out_scale):
        for hh in range(tn // HEAD_DIM):
            t = acc_scr[:, hh * HEAD_DIM:(hh + 1) * HEAD_DIM]
            ms = jnp.mean(t * t, axis=-1, keepdims=True)
            y = t * lax.rsqrt(ms + EPS) * gain_ref[...]
            r = y * cos_ref[...] + pltpu.roll(y, HEAD_DIM // 2, axis=1) * sin_ref[...]
            if out_scale != 1.0:
                r = r * out_scale
            o_ref[:, hh * HEAD_DIM:(hh + 1) * HEAD_DIM] = r.astype(o_ref.dtype)

    @pl.when(j < K_COL // tn)
    def _():
        head_norm_rope(qg_ref, HEAD_DIM ** -0.5 * LOG2E)

    @pl.when(jnp.logical_and(j >= K_COL // tn, j < V_COL // tn))
    def _():
        head_norm_rope(kg_ref, 1.0)

    @pl.when(jnp.logical_and(j >= V_COL // tn, j < G_COL // tn))
    def _():
        o_ref[...] = acc_scr[...].astype(o_ref.dtype)


def _proj(x2, g, sh, sc, w_in, w_gate, b_gate, cos_t, sin_t, qg, kg):
    s, d = x2.shape
    tm, tn = 1024, 1024
    n_in = G_COL // tn
    vec = lambda: pl.BlockSpec((1, d), lambda i, j: (0, 0))
    return pl.pallas_call(
        functools.partial(_proj_kernel, tn=tn),
        out_shape=jax.ShapeDtypeStruct((s, PROJ_W), jnp.bfloat16),
        grid=(s // tm, PROJ_W // tn),
        in_specs=[pl.BlockSpec((tm, d), lambda i, j: (i, 0)),
                  vec(), vec(), vec(),
                  pl.BlockSpec((d, tn), lambda i, j: (0, jnp.minimum(j, n_in - 1))),
                  pl.BlockSpec((d, tn), lambda i, j: (0, jnp.maximum(j - n_in, 0))),
                  pl.BlockSpec((1, tn), lambda i, j: (0, jnp.maximum(j - n_in, 0))),
                  pl.BlockSpec((tm, HEAD_DIM), lambda i, j: (i, 0)),
                  pl.BlockSpec((tm, HEAD_DIM), lambda i, j: (i, 0)),
                  pl.BlockSpec((1, HEAD_DIM), lambda i, j: (0, 0)),
                  pl.BlockSpec((1, HEAD_DIM), lambda i, j: (0, 0))],
        out_specs=pl.BlockSpec((tm, tn), lambda i, j: (i, j)),
        scratch_shapes=[pltpu.VMEM((tm, d), jnp.bfloat16),
                        pltpu.VMEM((tm, tn), jnp.float32)],
        compiler_params=_params(("arbitrary", "arbitrary"), 56),
        name="proj",
    )(x2, g, sh, sc, w_in, w_gate, b_gate, cos_t, sin_t, qg, kg)


def _select_kernel(q_ref, k_ref, wu_ref, wd_ref, bias_ref, wu_out, wd_out, km_scr, *, tq):
    i = pl.program_id(1)
    nb = k_ref.shape[0] // MOBA_BLOCK
    wu_out[...] = wu_ref[...].astype(wu_out.dtype)
    wd_out[...] = wd_ref[...].astype(wd_out.dtype)

    @pl.when(i == 0)
    def _():
        k = k_ref[...].astype(jnp.float32).reshape(nb, MOBA_BLOCK, HEAD_DIM)
        km = jnp.sum(k, axis=1) * (1.0 / MOBA_BLOCK)
        p0 = km.astype(jnp.bfloat16)
        r1 = km - p0.astype(jnp.float32)
        p1 = r1.astype(jnp.bfloat16)
        p2 = (r1 - p1.astype(jnp.float32)).astype(jnp.bfloat16)
        km_scr[0] = p0
        km_scr[1] = p1
        km_scr[2] = p2

    q = q_ref[...]
    gate = _dot_nt(km_scr[0], q) + _dot_nt(km_scr[1], q) + _dot_nt(km_scr[2], q)
    blk = lax.broadcasted_iota(jnp.int32, gate.shape, 0)
    qpos = i * tq + lax.broadcasted_iota(jnp.int32, gate.shape, 1)
    own = lax.shift_right_logical(qpos, BLOCK_SHIFT)
    past = blk < own
    g = jnp.where(past, gate, NEG)
    bias = jnp.where(blk == own, 0.0, NEG)
    for _ in range(MOBA_TOPK):
        top = jnp.max(g, axis=0, keepdims=True)
        first = jnp.min(jnp.where(g == top, blk, nb), axis=0, keepdims=True)
        pick = blk == first
        bias = jnp.where(jnp.logical_and(pick, past), 0.0, bias)
        g = jnp.where(pick, -jnp.inf, g)
    bias_ref[...] = bias


def _eye(n):
    return (lax.broadcasted_iota(jnp.int32, (n, n), 0)
            == lax.broadcasted_iota(jnp.int32, (n, n), 1)).astype(jnp.bfloat16)


def _row_slab_spec(w, lead, *, n_outer, n_inner):
    rows, cols = w.shape[-2] // (n_outer * n_inner), w.shape[-1]
    if lead:
        return pl.BlockSpec((None, rows, cols), lambda h, i: (0, h * n_inner + i, 0))
    return pl.BlockSpec((rows, cols), lambda h, i: (h * n_inner + i, 0))


def _select(proj, w_up, w_down):
    s = proj.shape[0]
    tq = 4096
    kcol = K_COL // HEAD_DIM
    n_i = s // tq
    slab = functools.partial(_row_slab_spec, n_outer=N_HEADS, n_inner=n_i)
    return pl.pallas_call(
        functools.partial(_select_kernel, tq=tq),
        out_shape=[jax.ShapeDtypeStruct((N_HEADS, s // MOBA_BLOCK, s), jnp.float32),
                   jax.ShapeDtypeStruct(w_up.shape[1:], jnp.bfloat16),
                   jax.ShapeDtypeStruct(w_down.shape[1:], jnp.bfloat16)],
        grid=(N_HEADS, n_i),
        in_specs=[pl.BlockSpec((tq, HEAD_DIM), lambda h, i: (i, h)),
                  pl.BlockSpec((s, HEAD_DIM), lambda h, i: (0, kcol + h)),
                  slab(w_up, True), slab(w_down, True)],
        out_specs=[pl.BlockSpec((None, s // MOBA_BLOCK, tq), lambda h, i: (h, 0, i)),
                   slab(w_up, False), slab(w_down, False)],
        scratch_shapes=[pltpu.VMEM((3, s // MOBA_BLOCK, HEAD_DIM), jnp.bfloat16)],
        compiler_params=_params(("arbitrary", "arbitrary"), 48),
        name="select",
    )(proj, proj, w_up, w_down)


ATTN_QB = 8
ATTN_UNROLL = 8
ONES_ROWS = 16
XPOSE_CHUNK = 2048


def _attn_kernel(q_ref, bias_ref, k_ref, v_ref, wa_ref, wp_ref, wo_ref, o_ref, wa_out, wp_out, wo_out,
                 vaug_scr, acc_scr, s_scr, p_scr, ptail_scr):
    i = pl.program_id(1)
    for w_ref, w_out in ((wa_ref, wa_out), (wp_ref, wp_out), (wo_ref, wo_out)):
        w_out[...] = w_ref[...].astype(w_out.dtype)
    s_len = k_ref.shape[0]
    blk = MOBA_BLOCK

    @pl.when(i == 0)
    def _():
        for c in range(s_len // XPOSE_CHUNK):
            keys = slice(c * XPOSE_CHUNK, (c + 1) * XPOSE_CHUNK)
            vaug_scr[:HEAD_DIM, keys] = _dot_nt(_eye(HEAD_DIM), v_ref[keys, :]).astype(vaug_scr.dtype)
        vaug_scr[HEAD_DIM:, :] = jnp.ones((ONES_ROWS, s_len), vaug_scr.dtype)

    ng = ATTN_QB
    n_past = i * ng
    groups = [slice(g * blk, (g + 1) * blk) for g in range(ng)]
    key_le_query = (lax.broadcasted_iota(jnp.int32, (blk, blk), 0)
                    <= lax.broadcasted_iota(jnp.int32, (blk, blk), 1))

    def score_tile(j, slot, g, causal):
        kb = k_ref[pl.ds(pl.multiple_of(j * blk, blk), blk), :]
        s_t = _dot_nt(kb, q_ref[groups[g], :])
        if causal:
            s_t = jnp.where(key_le_query, s_t, NEG)
        s_scr[slot, :, groups[g]] = s_t
        return jnp.max(s_t, axis=0, keepdims=True)

    def softmax_tile(j, slot, g, m_old, mx):
        b = bias_ref[pl.ds(j, 1), groups[g]]
        m_new = jnp.maximum(m_old, mx + b)
        shift = jnp.maximum(m_new - b, mx)
        p_scr[slot, :, groups[g]] = jnp.exp2(s_scr[slot, :, groups[g]] - shift).astype(p_scr.dtype)
        return m_new, jnp.exp2(m_old - m_new)

    def value_tile(j, slot, g, alpha):
        vb = vaug_scr[:, pl.ds(pl.multiple_of(j * blk, blk), blk)]
        acc_scr[:, groups[g]] = alpha * acc_scr[:, groups[g]] + _dot(vb, p_scr[slot, :, groups[g]])

    def past_step(j, slot, m, mx, a_prev):
        mx_next, m_alpha = [], []
        for g in range(ng):
            value_tile(jnp.maximum(j - 1, 0), 1 - slot, g, a_prev[g])
            mx_next.append(score_tile(j + 1, 1 - slot, g, False))
            m_alpha.append(softmax_tile(j, slot, g, m[g], mx[g]))
        mx_next = tuple(mx_next)
        return tuple(ma[0] for ma in m_alpha), mx_next, tuple(ma[1] for ma in m_alpha)

    def body(jj, carry):
        for u in range(ATTN_UNROLL):
            carry = past_step(ATTN_UNROLL * jj + u, u % 2, *carry)
        return carry

    acc_scr[...] = jnp.zeros_like(acc_scr)
    p_scr[1] = jnp.zeros(p_scr.shape[1:], p_scr.dtype)
    init = (tuple(jnp.full((1, blk), NEG, jnp.float32) for _ in range(ng)),
            tuple(score_tile(0, 0, g, False) for g in range(ng)),
            tuple(jnp.ones((1, blk), jnp.float32) for _ in range(ng)))
    m, mx, alpha = lax.fori_loop(0, n_past // ATTN_UNROLL, body, init)
    m, mx, alpha = list(m), list(mx), list(alpha)

    for g in range(ng):
        value_tile(jnp.maximum(n_past - 1, 0), 1, g, alpha[g])
    mxs = {(g, 0): mx[g] for g in range(1, ng)}
    mxs[(0, 0)] = score_tile(n_past, 0, 0, True)
    for t in range(1, ng):
        for g in range(t, ng):
            mxs[(g, t)] = score_tile(n_past + t, t, g, g == t)
    span0 = pl.multiple_of(n_past * blk, blk)
    for g in range(ng):
        bs = [bias_ref[pl.ds(n_past + t, 1), groups[g]] for t in range(g + 1)]
        m_new = m[g]
        for t in range(g + 1):
            m_new = jnp.maximum(m_new, mxs[(g, t)] + bs[t])
        for t in range(g + 1):
            shift = jnp.maximum(m_new - bs[t], mxs[(g, t)])
            ptail_scr[t * blk:(t + 1) * blk, groups[g]] = (
                jnp.exp2(s_scr[t, :, groups[g]] - shift).astype(ptail_scr.dtype))
        pv = _dot(vaug_scr[:, pl.ds(span0, (g + 1) * blk)], ptail_scr[:(g + 1) * blk, groups[g]])
        acc_scr[:, groups[g]] = jnp.exp2(m[g] - m_new) * acc_scr[:, groups[g]] + pv
    o_ref[...] = (acc_scr[:HEAD_DIM, :] / acc_scr[HEAD_DIM:HEAD_DIM + 1, :]).T.astype(o_ref.dtype)


def _attention(proj, bias, w_attn_br, w_pool_br, w_o):
    s = proj.shape[0]
    tq = ATTN_QB * MOBA_BLOCK
    kcol, vcol = K_COL // HEAD_DIM, V_COL // HEAD_DIM
    weights = (w_attn_br, w_pool_br, w_o)
    slab = functools.partial(_row_slab_spec, n_outer=N_HEADS, n_inner=s // tq)
    return pl.pallas_call(
        _attn_kernel,
        out_shape=[jax.ShapeDtypeStruct((s, ATTN_W), jnp.bfloat16)]
                  + [jax.ShapeDtypeStruct(w.shape[1:], jnp.bfloat16) for w in weights],
        grid=(N_HEADS, s // tq),
        in_specs=[pl.BlockSpec((tq, HEAD_DIM), lambda h, i: (i, h)),
                  pl.BlockSpec((None, s // MOBA_BLOCK, tq), lambda h, i: (h, 0, i)),
                  pl.BlockSpec((s, HEAD_DIM), lambda h, i: (0, kcol + h)),
                  pl.BlockSpec((s, HEAD_DIM), lambda h, i: (0, vcol + h))]
                 + [slab(w, True) for w in weights],
        out_specs=[pl.BlockSpec((tq, HEAD_DIM), lambda h, i: (i, h))] + [slab(w, False) for w in weights],
        scratch_shapes=[pltpu.VMEM((HEAD_DIM + ONES_ROWS, s), jnp.bfloat16),
                        pltpu.VMEM((HEAD_DIM + ONES_ROWS, tq), jnp.float32),
                        pltpu.VMEM((ATTN_QB, MOBA_BLOCK, tq), jnp.float32),
                        pltpu.VMEM((2, MOBA_BLOCK, tq), jnp.bfloat16),
                        pltpu.VMEM((tq, tq), jnp.bfloat16)],
        compiler_params=_params(("arbitrary", "arbitrary"), 56),
        name="attention",
    )(proj, bias, proj, proj, *weights)


POOL_HALO = 16


def _pool_kernel(u_ref, halo_ref, w_ref, ls_ref, o_ref):
    i = pl.program_id(0)
    tm = u_ref.shape[0]
    u = u_ref[...].astype(jnp.float32)
    halo = jnp.where(i == 0, 0.0, halo_ref[...].astype(jnp.float32))
    t = i * tm + lax.broadcasted_iota(jnp.int32, (tm, 1), 0)
    for g, win in enumerate(POOL_WINDOWS):
        cols = slice(g * POOL_GW, (g + 1) * POOL_GW)
        ug = u[:, cols]
        e = jnp.concatenate([halo[:, cols], ug], axis=0)
        span = 1
        while span < win:
            e = e + pltpu.roll(e, span, axis=0)
            span *= 2
        cnt = jnp.minimum(t + 1, win).astype(jnp.float32)
        dg = e[POOL_HALO:, :] / cnt - ug
        mixed = _dot(dg.astype(jnp.bfloat16), w_ref[g])
        o_ref[:, cols] = (mixed * ls_ref[:, cols]).astype(o_ref.dtype)


def _pool(proj, w_grp, ls):
    s = proj.shape[0]
    tm = 1024
    ucol = U_COL // POOL_W
    return pl.pallas_call(
        _pool_kernel,
        out_shape=jax.ShapeDtypeStruct((s, POOL_W), jnp.bfloat16),
        grid=(s // tm,),
        in_specs=[pl.BlockSpec((tm, POOL_W), lambda i: (i, ucol)),
                  pl.BlockSpec((POOL_HALO, POOL_W),
                               lambda i: (jnp.maximum(i * (tm // POOL_HALO) - 1, 0), ucol)),
                  pl.BlockSpec(w_grp.shape, lambda i: (0, 0, 0)),
                  pl.BlockSpec((1, POOL_W), lambda i: (0, 0))],
        out_specs=pl.BlockSpec((tm, POOL_W), lambda i: (i, 0)),
        compiler_params=_params(("arbitrary",), 48),
        name="pool",
    )(proj, proj, w_grp, ls)


def _merge_kernel(a_ref, p_ref, ga_ref, gp_ref, x_ref, g1_ref, wa_ref, wp_ref, wo_ref, o_ref):
    j = pl.program_id(1)

    @pl.when(j == 0)
    def _():
        o_ref[...] = jnp.zeros_like(o_ref)

    ya = _dot(a_ref[...], wa_ref[...])
    yp = _dot(p_ref[...], wp_ref[...])
    merged = ga_ref[...].astype(jnp.float32) * ya + gp_ref[...].astype(jnp.float32) * yp
    o_ref[...] += _dot(merged.astype(jnp.bfloat16), wo_ref[...])

    @pl.when(j == pl.num_programs(1) - 1)
    def _():
        o_ref[...] = x_ref[...] + g1_ref[...] * o_ref[...]


def _merge(attn, pooled, proj, x2, g1, wa, wp, wo):
    s, d = x2.shape
    tm, tn = 512, 1024
    ga0, gp0 = G_COL // tn, (G_COL + d) // tn
    return pl.pallas_call(
        _merge_kernel,
        out_shape=jax.ShapeDtypeStruct((s, d), jnp.float32),
        grid=(s // tm, d // tn),
        in_specs=[pl.BlockSpec((tm, ATTN_W), lambda i, j: (i, 0)),
                  pl.BlockSpec((tm, POOL_W), lambda i, j: (i, 0)),
                  pl.BlockSpec((tm, tn), lambda i, j: (i, ga0 + j)),
                  pl.BlockSpec((tm, tn), lambda i, j: (i, gp0 + j)),
                  pl.BlockSpec((tm, d), lambda i, j: (i, 0)),
                  pl.BlockSpec((1, d), lambda i, j: (0, 0)),
                  pl.BlockSpec((ATTN_W, tn), lambda i, j: (0, j)),
                  pl.BlockSpec((POOL_W, tn), lambda i, j: (0, j)),
                  pl.BlockSpec((tn, d), lambda i, j: (j, 0))],
        out_specs=pl.BlockSpec((tm, d), lambda i, j: (i, 0)),
        compiler_params=_params(("arbitrary", "arbitrary"), 56),
        name="merge",
    )(attn, pooled, proj, proj, x2, g1, wa, wp, wo)


FFN_CHUNK = 256


def _ffn_kernel(x_ref, g_ref, sh_ref, sc_ref, g2_ref, wua_ref, wub_ref, cw_ref, cb_ref, wdp_ref, wd_ref,
                o_ref, h_scr, tail_scr, upa_scr, upb_scr):
    i = pl.program_id(0)
    j = pl.program_id(1)
    tm = x_ref.shape[0]
    n_chunks = cw_ref.shape[0] // 2

    @pl.when(j == 0)
    def _():
        _norm_modulate_into(h_scr, x_ref, g_ref, sh_ref, sc_ref)
        o_ref[...] = jnp.zeros_like(o_ref)
        upa_scr[1] = jnp.zeros(upa_scr.shape[1:], upa_scr.dtype)
        upb_scr[1] = jnp.zeros(upb_scr.shape[1:], upb_scr.dtype)

    @pl.when(i == 0)
    def _():
        tail_scr[j] = jnp.zeros(tail_scr.shape[1:], tail_scr.dtype)

    chunks = [slice(c * FFN_CHUNK, (c + 1) * FFN_CHUNK) for c in range(2)]

    def up_stage(c):
        for half, (w_ref, u_scr) in enumerate(((wua_ref, upa_scr), (wub_ref, upb_scr))):
            u_scr[c, :SUBLANES, :] = tail_scr[j, half, :, chunks[c]]
            up = _dot(h_scr[...], w_ref[:, chunks[c]])
            u_scr[c, SUBLANES:, :] = up
            tail_scr[j, half, :, chunks[c]] = up[tm - SUBLANES:, :]

    def conv(ext, n):
        cw = cw_ref[n]
        out = cb_ref[n] + cw[CONV_W - 1:CONV_W, :] * ext[SUBLANES:, :]
        for back in range(1, CONV_W):
            tap = CONV_W - 1 - back
            out = out + cw[tap:tap + 1, :] * pltpu.roll(ext, back, axis=0)[SUBLANES:, :]
        return out

    def gated(c, n):
        a = conv(upa_scr[c], n)
        b = conv(upb_scr[c], n_chunks + n)
        return (a * _sigmoid(a) * b).astype(jnp.bfloat16)

    up_stage(0)
    act_prev = gated(1, jnp.maximum(2 * j - 1, 0))
    act_prev = jnp.where(j > 0, act_prev, jnp.zeros_like(act_prev))
    up_stage(1)
    act_0 = gated(0, 2 * j)
    o_ref[...] += _dot(act_prev, wdp_ref[chunks[1], :]) + _dot(act_0, wd_ref[chunks[0], :])

    @pl.when(j == pl.num_programs(1) - 1)
    def _():
        o_ref[...] += _dot(gated(1, 2 * j + 1), wd_ref[chunks[1], :])
        o_ref[...] = x_ref[...] + g2_ref[...] * o_ref[...]


def _ffn(x1, g, sh, sc, g2, w_up, conv_w, conv_b, w_down):
    s, d = x1.shape
    tm, tf = 512, 2 * FFN_CHUNK
    nf = D_FF // tf
    n_chunks = 2 * D_FF // FFN_CHUNK
    cw = conv_w.reshape(CONV_W, n_chunks, FFN_CHUNK).transpose(1, 0, 2)
    cb = conv_b.reshape(n_chunks, 1, FFN_CHUNK)
    vec = lambda: pl.BlockSpec((1, d), lambda i, j: (0, 0))
    return pl.pallas_call(
        _ffn_kernel,
        out_shape=jax.ShapeDtypeStruct((s, d), jnp.float32),
        grid=(s // tm, nf),
        in_specs=[pl.BlockSpec((tm, d), lambda i, j: (i, 0)),
                  vec(), vec(), vec(), vec(),
                  pl.BlockSpec((d, tf), lambda i, j: (0, j)),
                  pl.BlockSpec((d, tf), lambda i, j: (0, nf + j)),
                  pl.BlockSpec(cw.shape, lambda i, j: (0, 0, 0)),
                  pl.BlockSpec(cb.shape, lambda i, j: (0, 0, 0)),
                  pl.BlockSpec((tf, d), lambda i, j: (jnp.maximum(j - 1, 0), 0)),
                  pl.BlockSpec((tf, d), lambda i, j: (j, 0))],
        out_specs=pl.BlockSpec((tm, d), lambda i, j: (i, 0)),
        scratch_shapes=[pltpu.VMEM((tm, d), jnp.bfloat16),
                        pltpu.VMEM((nf, 2, SUBLANES, tf), jnp.float32),
                        pltpu.VMEM((2, SUBLANES + tm, FFN_CHUNK), jnp.float32),
                        pltpu.VMEM((2, SUBLANES + tm, FFN_CHUNK), jnp.float32)],
        compiler_params=_params(("arbitrary", "arbitrary"), 48),
        name="ffn",
    )(x1, g, sh, sc, g2, w_up, w_up, cw, cb, w_down, w_down)


ROPE_LO = 128


def _rope_tables(s):
    half = HEAD_DIM // 2
    inv = ROPE_THETA ** (-jnp.arange(half, dtype=jnp.float32) / half)
    lo = jnp.arange(ROPE_LO).astype(jnp.float32)[:, None] * inv[None, :]
    hi = (jnp.arange(s // ROPE_LO) * ROPE_LO).astype(jnp.float32)[:, None] * inv[None, :]
    cl, sl = jnp.cos(lo)[None], jnp.sin(lo)[None]
    ch, sh = jnp.cos(hi)[:, None], jnp.sin(hi)[:, None]
    cos = (ch * cl - sh * sl).reshape(s, half)
    sin = (sh * cl + ch * sl).reshape(s, half)
    return jnp.concatenate([cos, cos], axis=1), jnp.concatenate([-sin, sin], axis=1)


def kernel(x, c, w_ada, b_ada, norm_mix_g, w_in, q_norm_g, k_norm_g, w_pool_grp, pool_scale, w_attn_br,
           w_pool_br, w_gate, b_gate, w_o, norm_ffn_g, w_up, conv_w, conv_b, w_down):
    batch, s, d = x.shape
    assert batch == 1 and d == D_MODEL and w_ada.shape[0] == 1
    bf = jnp.bfloat16
    row = lambda v: v.reshape(1, -1)
    cos_t, sin_t = _rope_tables(s)

    mod = _ada(c.reshape(d, 1), w_ada[0], row(b_ada[0]))
    sh1, sc1, g1, sh2, sc2, g2 = [mod[:, n * d:(n + 1) * d] for n in range(6)]

    x2 = x[0]
    proj = _proj(x2, row(norm_mix_g[0]), sh1, sc1, w_in[0].astype(bf), w_gate[0].astype(bf), row(b_gate[0]),
                 cos_t, sin_t, row(q_norm_g[0]), row(k_norm_g[0]))
    bias, w_up_bf, w_down_bf = _select(proj, w_up, w_down)
    attn, wa_bf, wp_bf, wo_bf = _attention(proj, bias, w_attn_br, w_pool_br, w_o)
    pooled = _pool(proj, w_pool_grp[0].astype(bf), row(pool_scale[0]))
    x1 = _merge(attn, pooled, proj, x2, g1, wa_bf, wp_bf, wo_bf)
    out = _ffn(x1, row(norm_ffn_g[0]), sh2, sc2, g2, w_up_bf, conv_w[0], conv_b[0], w_down_bf)
    return out[None]
```
